```python
import math
import jax
import jax.numpy as jnp
from jax import lax
import numpy as np

D_MODEL = 4096
BATCH = 4
SEQ = 2048
DEPTH = 1
DEC_BATCH = 128
DEC_SEQ = 4
PAST_LEN = 16384
PAGE_SIZE = 128

MOBA_HEADS = 12
MOBA_KV_HEADS = 2
MOBA_HEAD_DIM = 128
MOBA_BLOCK = 256
MOBA_TOPK = 3
MOBA_QCHUNK = 64
MLA_HEADS = 12
MLA_Q_RANK = 768
MLA_KV_RANK = 512
MLA_NOPE_DIM = 128
MLA_ROPE_DIM = 64
MLA_QK_DIM = MLA_NOPE_DIM + MLA_ROPE_DIM
MLA_V_DIM = 128
MLA_LATENT_DIM = MLA_KV_RANK + MLA_ROPE_DIM
ROPE_THETA = 10000.0
ATTN_QBLOCK = 128
MEM_TOKENS = 256
MEM_HEADS = 4
MEM_HEAD_DIM = 256
PEER_HEADS = 8
PEER_N_KEYS = 128
PEER_N_EXPERTS = PEER_N_KEYS * PEER_N_KEYS
PEER_QUERY_DIM = 256
PEER_TOPK = 16
PEER_CHUNK = 64

N_BRANCHES = 3
RMS_EPS = 1e-6
IN_SPLITS = (MOBA_HEADS * MOBA_HEAD_DIM, MOBA_KV_HEADS * MOBA_HEAD_DIM, MOBA_KV_HEADS * MOBA_HEAD_DIM,
             MLA_Q_RANK, MLA_KV_RANK, MLA_ROPE_DIM, MEM_HEADS * MEM_HEAD_DIM, N_BRANCHES * D_MODEL)
IN_COLS = sum(IN_SPLITS)

kernel_name = "hybrid_moba_mla_mem_peer_step"


def rms_norm(x, g):
    xf = x.astype(jnp.float32)
    y = xf * lax.rsqrt(jnp.mean(xf * xf, axis=-1, keepdims=True) + RMS_EPS)
    return (y * g.astype(jnp.float32)).astype(x.dtype)


def alibi_slopes(n):
    def pow2(m):
        start = 2.0 ** (-8.0 / m)
        return [start ** (i + 1) for i in range(m)]
    p = 2 ** int(math.floor(math.log2(n)))
    s = pow2(p)
    if p < n:
        s = s + pow2(2 * p)[0::2][: n - p]
    return jnp.asarray(np.array(s, dtype=np.float32))


def rope(x, pos):
    half = MLA_ROPE_DIM // 2
    inv = ROPE_THETA ** (-jnp.arange(half, dtype=jnp.float32) / half)
    ang = pos.astype(jnp.float32)[:, None] * inv
    cos, sin = jnp.cos(ang)[:, None, :], jnp.sin(ang)[:, None, :]
    xf = x.astype(jnp.float32)
    x1, x2 = xf[..., :half], xf[..., half:]
    return jnp.concatenate([x1 * cos - x2 * sin, x2 * cos + x1 * sin], axis=-1).astype(x.dtype)


def mla_queries(cq, pos, w_mla_uq, mla_q_norm_g):
    q = (cq @ w_mla_uq).reshape(*cq.shape[:-1], MLA_HEADS, MLA_QK_DIM)
    q = rms_norm(q, mla_q_norm_g)
    return jnp.concatenate([q[..., :MLA_NOPE_DIM], rope(q[..., MLA_NOPE_DIM:], pos)], axis=-1)


def mla_keys(c, kpe, pos, w_mla_uk, mla_k_norm_g):
    lead = c.shape[:-1]
    kn = (c @ w_mla_uk).reshape(*lead, MLA_HEADS, MLA_NOPE_DIM)
    kr = jnp.broadcast_to(kpe[..., None, :], (*lead, MLA_HEADS, MLA_ROPE_DIM)).astype(kn.dtype)
    k = rms_norm(jnp.concatenate([kn, kr], axis=-1), mla_k_norm_g)
    return jnp.concatenate([k[..., :MLA_NOPE_DIM], rope(k[..., MLA_NOPE_DIM:], pos)], axis=-1)


def attn_inputs(h, pos, w_in, moba_q_norm_g, moba_k_norm_g, mla_cq_norm_g, mla_ckv_norm_g,
                w_mla_uq, mla_q_norm_g, mem_q_norm_g):
    lead = h.shape[:-1]
    cuts = [int(c) for c in np.cumsum(IN_SPLITS)[:-1]]
    qa, ka, va, cq, ckv, kpe, qm, gates = jnp.split(h @ w_in, cuts, axis=-1)
    qa = rms_norm(qa.reshape(*lead, MOBA_HEADS, MOBA_HEAD_DIM), moba_q_norm_g)
    ka = rms_norm(ka.reshape(*lead, MOBA_KV_HEADS, MOBA_HEAD_DIM), moba_k_norm_g)
    va = va.reshape(*lead, MOBA_KV_HEADS, MOBA_HEAD_DIM)
    qb = mla_queries(rms_norm(cq, mla_cq_norm_g), pos, w_mla_uq, mla_q_norm_g)
    ckv = rms_norm(ckv, mla_ckv_norm_g)
    qm = rms_norm(qm.reshape(*lead, MEM_HEADS, MEM_HEAD_DIM), mem_q_norm_g)
    return qa, ka, va, qb, ckv, kpe, qm, gates


def moba_attend(qc, tq, ksel, vsel, s_past, valid, kown, vown, s_own, slopes):
    Q, H, J, L, _ = ksel.shape
    scale = MOBA_HEAD_DIM ** -0.5
    lp = (jnp.einsum('qhd,qhjld->qhjl', qc, ksel).astype(jnp.float32) * scale
          - slopes[None, :, None, None] * (tq[:, None, None, None] - s_past).astype(jnp.float32))
    lp = jnp.where(valid[..., None], lp, -jnp.inf).reshape(Q, H, J * L)
    lo = (jnp.einsum('qhd,mhd->qhm', qc, kown).astype(jnp.float32) * scale
          - slopes[None, :, None] * (tq[:, None, None] - s_own[None, None, :]).astype(jnp.float32))
    lo = jnp.where(s_own[None, None, :] <= tq[:, None, None], lo, -jnp.inf)
    p = jax.nn.softmax(jnp.concatenate([lp, lo], axis=-1), axis=-1)
    out = (jnp.einsum('qhjl,qhjld->qhd', p[..., :J * L].reshape(Q, H, J, L), vsel)
           + jnp.einsum('qhm,mhd->qhd', p[..., J * L:], vown))
    return out.astype(qc.dtype)


def moba_prompt(q, k, v, slopes):
    B, S = q.shape[:2]
    nblk = -(-S // MOBA_BLOCK)
    pad = nblk * MOBA_BLOCK - S
    kp = jnp.pad(k, ((0, 0), (0, pad), (0, 0), (0, 0))).reshape(B, nblk, MOBA_BLOCK, MOBA_KV_HEADS, MOBA_HEAD_DIM)
    vp = jnp.pad(v, ((0, 0), (0, pad), (0, 0), (0, 0))).reshape(B, nblk, MOBA_BLOCK, MOBA_KV_HEADS, MOBA_HEAD_DIM)
    kvh = jnp.arange(MOBA_HEADS) // (MOBA_HEADS // MOBA_KV_HEADS)
    kmean = jnp.mean(kp.astype(jnp.float32), axis=2)[:, :, kvh, :]
    gate = jnp.einsum('bshd,bnhd->bshn', q.astype(jnp.float32), kmean)
    own = jnp.arange(S) // MOBA_BLOCK
    past = jnp.arange(nblk)[None, :] < own[:, None]
    gate = jnp.where(past[None, :, None, :], gate, -jnp.inf)
    gate = jnp.pad(gate, ((0, 0), (0, 0), (0, 0), (0, max(MOBA_TOPK - nblk, 0))), constant_values=-jnp.inf)
    _, sel = lax.top_k(gate, MOBA_TOPK)
    valid = sel < own[None, :, None, None]
    sel = jnp.minimum(sel, nblk - 1)
    nq = S // MOBA_QCHUNK
    kvh_idx = kvh[None, :, None]

    def chunk(i):
        b = i // nq
        t0 = (i % nq) * MOBA_QCHUNK
        qc = lax.dynamic_slice(q, (b, t0, 0, 0), (1, MOBA_QCHUNK, MOBA_HEADS, MOBA_HEAD_DIM))[0]
        sc = lax.dynamic_slice(sel, (b, t0, 0, 0), (1, MOBA_QCHUNK, MOBA_HEADS, MOBA_TOPK))[0]
        vc = lax.dynamic_slice(valid, (b, t0, 0, 0), (1, MOBA_QCHUNK, MOBA_HEADS, MOBA_TOPK))[0]
        kb, vb = kp[b], vp[b]
        ksel = kb[sc, :, kvh_idx, :]
        vsel = vb[sc, :, kvh_idx, :]
        ob = t0 // MOBA_BLOCK
        kown = lax.dynamic_index_in_dim(kb, ob, 0, keepdims=False)[:, kvh, :]
        vown = lax.dynamic_index_in_dim(vb, ob, 0, keepdims=False)[:, kvh, :]
        tq = t0 + jnp.arange(MOBA_QCHUNK)
        s_past = sc[..., None] * MOBA_BLOCK + jnp.arange(MOBA_BLOCK)
        s_own = ob * MOBA_BLOCK + jnp.arange(MOBA_BLOCK)
        return moba_attend(qc, tq, ksel, vsel, s_past, vc, kown, vown, s_own, slopes)

    out = lax.map(chunk, jnp.arange(B * nq))
    return out.reshape(B, S, MOBA_HEADS * MOBA_HEAD_DIM)


def moba_sample(q, k_new, v_new, k_pool, v_pool, page_table, slopes):
    DB, Q = q.shape[:2]
    n_pages = PAST_LEN // PAGE_SIZE
    bp = MOBA_BLOCK // PAGE_SIZE
    nb = PAST_LEN // MOBA_BLOCK
    own_start = nb * MOBA_BLOCK
    n_own_past = PAST_LEN - own_start
    nb_pad = max(nb, MOBA_TOPK)
    kvh = jnp.arange(MOBA_HEADS) // (MOBA_HEADS // MOBA_KV_HEADS)
    tq = PAST_LEN + jnp.arange(Q)
    s_own = jnp.concatenate([own_start + jnp.arange(n_own_past), tq])

    def one_seq(args):
        qb, kn, vn, pt = args
        kpast = k_pool[pt].reshape(PAST_LEN, MOBA_KV_HEADS, MOBA_HEAD_DIM)
        kmean = kpast[:nb * MOBA_BLOCK].reshape(nb, MOBA_BLOCK, MOBA_KV_HEADS, MOBA_HEAD_DIM).astype(jnp.float32).mean(1)[:, kvh, :]
        gate = jnp.einsum('qhd,nhd->qhn', qb.astype(jnp.float32), kmean)
        gate = jnp.pad(gate, ((0, 0), (0, 0), (0, nb_pad - nb)), constant_values=-jnp.inf)
        _, sel = lax.top_k(gate, MOBA_TOPK)
        valid = sel < nb
        sel = jnp.minimum(sel, max(nb - 1, 0))
        lpage = jnp.minimum(sel[..., None] * bp + jnp.arange(bp), n_pages - 1)
        phys = pt[lpage]
        kidx = kvh[None, :, None, None]
        ksel = k_pool[phys, :, kidx, :].reshape(Q, MOBA_HEADS, MOBA_TOPK, MOBA_BLOCK, MOBA_HEAD_DIM)
        vsel = v_pool[phys, :, kidx, :].reshape(Q, MOBA_HEADS, MOBA_TOPK, MOBA_BLOCK, MOBA_HEAD_DIM)
        s_past = sel[..., None] * MOBA_BLOCK + jnp.arange(MOBA_BLOCK)
        v_own_past = v_pool[pt[nb * bp:]].reshape(n_own_past, MOBA_KV_HEADS, MOBA_HEAD_DIM)
        kown = jnp.concatenate([kpast[own_start:], kn], axis=0)[:, kvh, :]
        vown = jnp.concatenate([v_own_past, vn], axis=0)[:, kvh, :]
        return moba_attend(qb, tq, ksel, vsel, s_past, valid, kown, vown, s_own, slopes)

    out = lax.map(one_seq, (q, k_new, v_new, page_table))
    return out.reshape(DB, Q, MOBA_HEADS * MOBA_HEAD_DIM)


def mla_prompt(q, k, v):
    B, S = q.shape[:2]
    nq = S // ATTN_QBLOCK
    scale = MLA_QK_DIM ** -0.5
    kpos = jnp.arange(S)

    def block(i):
        t0 = i * ATTN_QBLOCK
        qc = lax.dynamic_slice_in_dim(q, t0, ATTN_QBLOCK, axis=1)
        tq = t0 + jnp.arange(ATTN_QBLOCK)
        lg = jnp.einsum('bqhd,bshd->bhqs', qc, k).astype(jnp.float32) * scale
        lg = jnp.where(kpos[None, :] <= tq[:, None], lg, -jnp.inf)
        p = jax.nn.softmax(lg, axis=-1)
        return jnp.einsum('bhqs,bshd->bqhd', p, v).astype(q.dtype)

    out = lax.map(block, jnp.arange(nq))
    return jnp.moveaxis(out, 0, 1).reshape(B, S, MLA_HEADS * MLA_V_DIM)


def online_softmax_step(carry, logits, vals):
    m, l, acc = carry
    m_new = jnp.maximum(m, jnp.max(logits, axis=-1))
    alpha = jnp.exp(m - m_new)
    p = jnp.exp(logits - m_new[..., None])
    return (m_new, l * alpha + jnp.sum(p, axis=-1),
            acc * alpha[..., None] + jnp.einsum('bhqs,bsc->bhqc', p, vals.astype(jnp.float32)))


def mla_sample(q, c_new, kpe_new, latent_pool, page_table, w_mla_uk, w_mla_uv, mla_k_norm_g):
    DB, Q = q.shape[:2]
    n_pages = PAST_LEN // PAGE_SIZE
    scale = MLA_QK_DIM ** -0.5
    qf = q.astype(jnp.float32)

    def page_step(carry, p):
        lat = latent_pool[page_table[:, p]]
        c, kpe = lat[..., :MLA_KV_RANK], lat[..., MLA_KV_RANK:]
        k = mla_keys(c, kpe, p * PAGE_SIZE + jnp.arange(PAGE_SIZE), w_mla_uk, mla_k_norm_g)
        lg = jnp.einsum('bqhd,bshd->bhqs', qf, k.astype(jnp.float32)) * scale
        return online_softmax_step(carry, lg, c), None

    init = (jnp.full((DB, MLA_HEADS, Q), -jnp.inf, jnp.float32),
            jnp.zeros((DB, MLA_HEADS, Q), jnp.float32),
            jnp.zeros((DB, MLA_HEADS, Q, MLA_KV_RANK), jnp.float32))
    carry, _ = lax.scan(page_step, init, jnp.arange(n_pages))
    tloc = jnp.arange(Q)
    kn = mla_keys(c_new, kpe_new, PAST_LEN + tloc, w_mla_uk, mla_k_norm_g)
    lg = jnp.einsum('bqhd,bshd->bhqs', qf, kn.astype(jnp.float32)) * scale
    lg = jnp.where(tloc[None, :] <= tloc[:, None], lg, -jnp.inf)
    m, l, acc = online_softmax_step(carry, lg, c_new)
    w_uv = w_mla_uv.reshape(MLA_KV_RANK, MLA_HEADS, MLA_V_DIM).astype(jnp.float32)
    out = jnp.einsum('bhqc,chd->bqhd', acc / l[..., None], w_uv)
    return out.reshape(DB, Q, MLA_HEADS * MLA_V_DIM).astype(q.dtype)


def mem_kv(mem, norm_mem_g, w_mem_k, w_mem_v, mem_k_norm_g):
    m = rms_norm(mem, norm_mem_g)
    lead = m.shape[:-1]
    k = rms_norm((m @ w_mem_k).reshape(*lead, MEM_HEADS, MEM_HEAD_DIM), mem_k_norm_g)
    v = (m @ w_mem_v).reshape(*lead, MEM_HEADS, MEM_HEAD_DIM)
    return k, v


def mem_attend(q, k, v):
    B, S = q.shape[:2]
    lg = jnp.einsum('bshd,bmhd->bhsm', q, k).astype(jnp.float32) * (MEM_HEAD_DIM ** -0.5)
    p = jax.nn.softmax(lg, axis=-1)
    return jnp.einsum('bhsm,bmhd->bshd', p, v).astype(q.dtype).reshape(B, S, MEM_HEADS * MEM_HEAD_DIM)


def gated_merge(gates, o_a, o_b, o_m, w_moba_o, w_mla_o, w_mem_o, w_out):
    g = jax.nn.sigmoid(gates.astype(jnp.float32)).astype(o_a.dtype)
    g = g.reshape(*gates.shape[:-1], N_BRANCHES, D_MODEL)
    m = g[..., 0, :] * (o_a @ w_moba_o) + g[..., 1, :] * (o_b @ w_mla_o) + g[..., 2, :] * (o_m @ w_mem_o)
    return m @ w_out


def peer(h, w_peer_q, peer_subkeys, peer_u, peer_v):
    lead = h.shape[:-1]
    x = h.reshape(-1, D_MODEL)
    T = x.shape[0]
    half = PEER_QUERY_DIM // 2
    q = (x @ w_peer_q).reshape(T, PEER_HEADS, 2, half)
    s = jnp.einsum('thpd,hpnd->thpn', q, peer_subkeys).astype(jnp.float32)
    sv, si = lax.top_k(s, PEER_TOPK)
    kk = PEER_TOPK * PEER_TOPK
    cand = (sv[:, :, 0, :, None] + sv[:, :, 1, None, :]).reshape(T, PEER_HEADS, kk)
    cidx = (si[:, :, 0, :, None] * PEER_N_KEYS + si[:, :, 1, None, :]).reshape(T, PEER_HEADS, kk)
    top_s, top_j = lax.top_k(cand, PEER_TOPK)
    eidx = jnp.take_along_axis(cidx, top_j, axis=-1)
    gate = jax.nn.softmax(top_s, axis=-1)
    pad = (-T) % PEER_CHUNK
    nc = (T + pad) // PEER_CHUNK
    xp = jnp.pad(x, ((0, pad), (0, 0))).reshape(nc, PEER_CHUNK, D_MODEL)
    ep = jnp.pad(eidx, ((0, pad), (0, 0), (0, 0))).reshape(nc, PEER_CHUNK, PEER_HEADS, PEER_TOPK)
    gp = jnp.pad(gate, ((0, pad), (0, 0), (0, 0))).reshape(nc, PEER_CHUNK, PEER_HEADS, PEER_TOPK)

    def chunk(args):
        xc, ec, gc = args
        act = jax.nn.gelu(jnp.einsum('cd,chkd->chk', xc, peer_u[ec]).astype(jnp.float32), approximate=False)
        return jnp.einsum('chk,chkd->cd', (gc * act).astype(xc.dtype), peer_v[ec])

    out = lax.map(chunk, (xp, ep, gp))
    return out.reshape(nc * PEER_CHUNK, D_MODEL)[:T].reshape(*lead, D_MODEL)


def run_layer(xp, xs, mem_prompt, c_moba_k, c_moba_v, c_mla_lat, c_mem_k, c_mem_v, page_table,
              norm_attn_g, norm_ffn_g, norm_mem_g, w_in, moba_q_norm_g, moba_k_norm_g,
              mla_cq_norm_g, mla_ckv_norm_g, w_mla_uq, w_mla_uk, w_mla_uv, mla_q_norm_g, mla_k_norm_g,
              w_mem_k, w_mem_v, mem_q_norm_g, mem_k_norm_g, w_moba_o, w_mla_o, w_mem_o, w_out,
              w_peer_q, peer_subkeys, peer_u, peer_v):
    slopes = alibi_slopes(MOBA_HEADS)
    B, S = xp.shape[:2]
    pos_p = jnp.arange(S)
    hp = rms_norm(xp, norm_attn_g)
    qa, ka, va, qb, ckv, kpe, qm, gates = attn_inputs(hp, pos_p, w_in, moba_q_norm_g, moba_k_norm_g, mla_cq_norm_g,
                                                      mla_ckv_norm_g, w_mla_uq, mla_q_norm_g, mem_q_norm_g)
    o_a = moba_prompt(qa, ka, va, slopes)
    kb = mla_keys(ckv, kpe, pos_p, w_mla_uk, mla_k_norm_g)
    vb = (ckv @ w_mla_uv).reshape(B, S, MLA_HEADS, MLA_V_DIM)
    o_b = mla_prompt(qb, kb, vb)
    mk, mv = mem_kv(mem_prompt, norm_mem_g, w_mem_k, w_mem_v, mem_k_norm_g)
    o_m = mem_attend(qm, mk, mv)
    xp = xp + gated_merge(gates, o_a, o_b, o_m, w_moba_o, w_mla_o, w_mem_o, w_out)
    xp = xp + peer(rms_norm(xp, norm_ffn_g), w_peer_q, peer_subkeys, peer_u, peer_v)
    lat_p = jnp.concatenate([ckv, kpe], axis=-1)
    Q = xs.shape[1]
    pos_s = PAST_LEN + jnp.arange(Q)
    hs = rms_norm(xs, norm_attn_g)
    qa_s, ka_s, va_s, qb_s, ckv_s, kpe_s, qm_s, gates_s = attn_inputs(hs, pos_s, w_in, moba_q_norm_g, moba_k_norm_g,
                                                                      mla_cq_norm_g, mla_ckv_norm_g, w_mla_uq,
                                                                      mla_q_norm_g, mem_q_norm_g)
    o_a_s = moba_sample(qa_s, ka_s, va_s, c_moba_k, c_moba_v, page_table, slopes)
    o_b_s = mla_sample(qb_s, ckv_s, kpe_s, c_mla_lat, page_table, w_mla_uk, w_mla_uv, mla_k_norm_g)
    o_m_s = mem_attend(qm_s, c_mem_k, c_mem_v)
    xs = xs + gated_merge(gates_s, o_a_s, o_b_s, o_m_s, w_moba_o, w_mla_o, w_mem_o, w_out)
    xs = xs + peer(rms_norm(xs, norm_ffn_g), w_peer_q, peer_subkeys, peer_u, peer_v)
    lat_s = jnp.concatenate([ckv_s, kpe_s], axis=-1)
    return xp, xs, (ka, va, lat_p, mk, mv, ka_s, va_s, lat_s)


def setup_inputs(seed: int = 0) -> dict:
    key = jax.random.key(seed)
    ks = jax.random.split(key, 34)
    f32 = jnp.float32

    def nrm(k, shape, scale):
        return jax.random.normal(k, shape, f32) * scale

    def gain(k, n):
        return 1.0 + 0.05 * jax.random.normal(k, (DEPTH, n), f32)

    n_pages = PAST_LEN // PAGE_SIZE
    n_used = DEC_BATCH * n_pages
    n_phys = (5 * n_used + 3) // 4
    page_table = jax.random.permutation(ks[8], n_phys)[:n_used].reshape(DEC_BATCH, n_pages).astype(jnp.int32)
    return {
        'x_prompt': nrm(ks[0], (BATCH, SEQ, D_MODEL), 1.0),
        'x_sample': nrm(ks[1], (DEC_BATCH, DEC_SEQ, D_MODEL), 1.0),
        'mem_prompt': nrm(ks[2], (BATCH, MEM_TOKENS, D_MODEL), 1.0),
        'cache_moba_k': nrm(ks[3], (DEPTH, n_phys, PAGE_SIZE, MOBA_KV_HEADS, MOBA_HEAD_DIM), 1.0),
        'cache_moba_v': nrm(ks[4], (DEPTH, n_phys, PAGE_SIZE, MOBA_KV_HEADS, MOBA_HEAD_DIM), 1.0),
        'cache_mla_latent': nrm(ks[5], (DEPTH, n_phys, PAGE_SIZE, MLA_LATENT_DIM), 1.0),
        'cache_mem_k': nrm(ks[6], (DEPTH, DEC_BATCH, MEM_TOKENS, MEM_HEADS, MEM_HEAD_DIM), 1.0),
        'cache_mem_v': nrm(ks[7], (DEPTH, DEC_BATCH, MEM_TOKENS, MEM_HEADS, MEM_HEAD_DIM), 1.0),
        'page_table': page_table,
        'norm_attn_g': gain(ks[9], D_MODEL),
        'norm_ffn_g': gain(ks[10], D_MODEL),
        'norm_mem_g': gain(ks[11], D_MODEL),
        'w_in': nrm(ks[12], (DEPTH, D_MODEL, IN_COLS), D_MODEL ** -0.5),
        'moba_q_norm_g': gain(ks[13], MOBA_HEAD_DIM),
        'moba_k_norm_g': gain(ks[14], MOBA_HEAD_DIM),
        'mla_cq_norm_g': gain(ks[15], MLA_Q_RANK),
        'mla_ckv_norm_g': gain(ks[16], MLA_KV_RANK),
        'w_mla_uq': nrm(ks[17], (DEPTH, MLA_Q_RANK, MLA_HEADS * MLA_QK_DIM), MLA_Q_RANK ** -0.5),
        'w_mla_uk': nrm(ks[18], (DEPTH, MLA_KV_RANK, MLA_HEADS * MLA_NOPE_DIM), MLA_KV_RANK ** -0.5),
        'w_mla_uv': nrm(ks[19], (DEPTH, MLA_KV_RANK, MLA_HEADS * MLA_V_DIM), MLA_KV_RANK ** -0.5),
        'mla_q_norm_g': gain(ks[20], MLA_QK_DIM),
        'mla_k_norm_g': gain(ks[21], MLA_QK_DIM),
        'w_mem_k': nrm(ks[22], (DEPTH, D_MODEL, MEM_HEADS * MEM_HEAD_DIM), D_MODEL ** -0.5),
        'w_mem_v': nrm(ks[23], (DEPTH, D_MODEL, MEM_HEADS * MEM_HEAD_DIM), D_MODEL ** -0.5),
        'mem_q_norm_g': gain(ks[24], MEM_HEAD_DIM),
        'mem_k_norm_g': gain(ks[25], MEM_HEAD_DIM),
        'w_moba_o': nrm(ks[26], (DEPTH, MOBA_HEADS * MOBA_HEAD_DIM, D_MODEL), (MOBA_HEADS * MOBA_HEAD_DIM) ** -0.5),
        'w_mla_o': nrm(ks[27], (DEPTH, MLA_HEADS * MLA_V_DIM, D_MODEL), (MLA_HEADS * MLA_V_DIM) ** -0.5),
        'w_mem_o': nrm(ks[28], (DEPTH, MEM_HEADS * MEM_HEAD_DIM, D_MODEL), (MEM_HEADS * MEM_HEAD_DIM) ** -0.5),
        'w_out': nrm(ks[29], (DEPTH, D_MODEL, D_MODEL), D_MODEL ** -0.5),
        'w_peer_q': nrm(ks[30], (DEPTH, D_MODEL, PEER_HEADS * PEER_QUERY_DIM), D_MODEL ** -0.5),
        'peer_subkeys': nrm(ks[31], (DEPTH, PEER_HEADS, 2, PEER_N_KEYS, PEER_QUERY_DIM // 2), (PEER_QUERY_DIM // 2) ** -0.5),
        'peer_u': nrm(ks[32], (DEPTH, PEER_N_EXPERTS, D_MODEL), D_MODEL ** -0.5),
        'peer_v': nrm(ks[33], (DEPTH, PEER_N_EXPERTS, D_MODEL), PEER_HEADS ** -0.5),
    }


def reference(x_prompt, x_sample, mem_prompt, cache_moba_k, cache_moba_v, cache_mla_latent, cache_mem_k,
              cache_mem_v, page_table, norm_attn_g, norm_ffn_g, norm_mem_g, w_in, moba_q_norm_g, moba_k_norm_g,
              mla_cq_norm_g, mla_ckv_norm_g, w_mla_uq, w_mla_uk, w_mla_uv, mla_q_norm_g, mla_k_norm_g,
              w_mem_k, w_mem_v, mem_q_norm_g, mem_k_norm_g, w_moba_o, w_mla_o, w_mem_o, w_out,
              w_peer_q, peer_subkeys, peer_u, peer_v):
    xp, xs = x_prompt, x_sample
    new = [[] for _ in range(8)]
    for l in range(DEPTH):
        xp, xs, states = run_layer(
            xp, xs, mem_prompt, cache_moba_k[l], cache_moba_v[l], cache_mla_latent[l], cache_mem_k[l],
            cache_mem_v[l], page_table, norm_attn_g[l], norm_ffn_g[l], norm_mem_g[l], w_in[l],
            moba_q_norm_g[l], moba_k_norm_g[l], mla_cq_norm_g[l], mla_ckv_norm_g[l], w_mla_uq[l], w_mla_uk[l],
            w_mla_uv[l], mla_q_norm_g[l], mla_k_norm_g[l], w_mem_k[l], w_mem_v[l], mem_q_norm_g[l],
            mem_k_norm_g[l], w_moba_o[l], w_mla_o[l], w_mem_o[l], w_out[l], w_peer_q[l], peer_subkeys[l],
            peer_u[l], peer_v[l])
        for lst, val in zip(new, states):
            lst.append(val)
    return (xp, xs, jnp.stack(new[0]), jnp.stack(new[1]), jnp.stack(new[2]), jnp.stack(new[3]),
            jnp.stack(new[4]), jnp.stack(new[5]), jnp.stack(new[6]), jnp.stack(new[7]))
```

```python
import functools
import math

import numpy as np
import jax
import jax.numpy as jnp
from jax import lax
from jax.experimental import pallas as pl
from jax.experimental.pallas import tpu as pltpu

MOBA_HEADS = 12
MOBA_KV_HEADS = 2
MOBA_HEAD_DIM = 128
MOBA_BLOCK = 256
MOBA_TOPK = 3
MLA_HEADS = 12
MLA_Q_RANK = 768
MLA_KV_RANK = 512
MLA_NOPE_DIM = 128
MLA_ROPE_DIM = 64
MLA_QK_DIM = MLA_NOPE_DIM + MLA_ROPE_DIM
MLA_V_DIM = 128
ROPE_THETA = 10000.0
MEM_HEADS = 4
MEM_HEAD_DIM = 256
PEER_HEADS = 8
PEER_N_KEYS = 128
PEER_QUERY_DIM = 256
PEER_TOPK = 16
RMS_EPS = 1e-6

LANES = 128
MLA_QK_PAD = 2 * LANES
VMEM_LIMIT = 56 * 1024 * 1024
NEG = -1e30
HI = lax.Precision.HIGHEST
F32 = jnp.float32
BF16 = jnp.bfloat16
NT_DIMS = (((1,), (1,)), ((), ()))


def _tile(n, pref, align=8):
    t = (min(pref, n) // align) * align
    while t >= align:
        if n % t == 0:
            return t
        t -= align
    return n


def _params(*sem):
    return pltpu.CompilerParams(dimension_semantics=sem, vmem_limit_bytes=VMEM_LIMIT)


def _nt(a, b, precision=None):
    return lax.dot_general(a, b, NT_DIMS, precision=precision, preferred_element_type=F32)


def _headnorm_kernel(x_ref, g_ref, o_ref, *, hd):
    g = g_ref[...]
    for h in range(x_ref.shape[1] // hd):
        x = x_ref[:, h * hd:(h + 1) * hd]
        inv = lax.rsqrt(jnp.mean(x * x, axis=-1, keepdims=True) + RMS_EPS)
        o_ref[:, h * hd:(h + 1) * hd] = (x * inv * g).astype(o_ref.dtype)


def _headnorm(x, g, hd, out_dtype):
    T, W = x.shape
    tm = _tile(T, 256)
    return pl.pallas_call(
        functools.partial(_headnorm_kernel, hd=hd),
        out_shape=jax.ShapeDtypeStruct((T, W), out_dtype),
        grid=(T // tm,),
        in_specs=[pl.BlockSpec((tm, W), lambda i: (i, 0)), pl.BlockSpec((1, hd), lambda i: (0, 0))],
        out_specs=pl.BlockSpec((tm, W), lambda i: (i, 0)),
        compiler_params=_params("parallel"),
        name="headnorm",
    )(x, g.reshape(1, hd).astype(F32))


def _mm_kernel(a_ref, w_ref, o_ref):
    o_ref[...] = jnp.dot(a_ref[...], w_ref[...], preferred_element_type=F32).astype(o_ref.dtype)


def _mm_res_kernel(a_ref, w_ref, r_ref, o_ref):
    o_ref[...] = r_ref[...] + jnp.dot(a_ref[...], w_ref[...], preferred_element_type=F32)


def _mm(a, w, *, residual=None, tm=512, tn=512, out_dtype=F32):
    M, K = a.shape
    N = w.shape[1]
    tm, tn = _tile(M, tm), _tile(N, tn, LANES)
    in_specs = [pl.BlockSpec((tm, K), lambda i, j: (i, 0)), pl.BlockSpec((K, tn), lambda i, j: (0, j))]
    args = [a, w]
    body = _mm_kernel
    if residual is not None:
        in_specs.append(pl.BlockSpec((tm, tn), lambda i, j: (i, j)))
        args.append(residual)
        body = _mm_res_kernel
    return pl.pallas_call(
        body,
        out_shape=jax.ShapeDtypeStruct((M, N), out_dtype),
        grid=(M // tm, N // tn),
        in_specs=in_specs,
        out_specs=pl.BlockSpec((tm, tn), lambda i, j: (i, j)),
        compiler_params=_params("parallel", "parallel"),
        name="mm",
    )(*args)


def _rope_tables(pos):
    half = MLA_ROPE_DIM // 2
    inv = ROPE_THETA ** (-jnp.arange(half, dtype=F32) / half)
    ang = pos.astype(F32)[:, None] * inv
    cos, sin = jnp.cos(ang), jnp.sin(ang)
    z = jnp.zeros_like(cos)
    return jnp.concatenate([cos, cos, z, z], axis=-1), jnp.concatenate([-sin, sin, z, z], axis=-1)


def _rope_dup(v):
    half = MLA_ROPE_DIM // 2
    return jnp.concatenate([v, v[..., half:], v[..., :half]], axis=-1)


def _mla_q_kernel(q_ref, t1_ref, t2_ref, gn_ref, g2_ref, o_ref):
    t1, t2, gn, g2 = t1_ref[...], t2_ref[...], gn_ref[...], g2_ref[...]
    first = lax.broadcasted_iota(jnp.int32, t1.shape, 1) < MLA_ROPE_DIM
    for h in range(MLA_HEADS):
        lo = h * MLA_QK_PAD
        n = q_ref[:, lo:lo + LANES]
        r = q_ref[:, lo + LANES:lo + MLA_QK_PAD]
        ss = (jnp.sum(n * n, axis=-1, keepdims=True)
              + jnp.sum(jnp.where(first, r * r, 0.0), axis=-1, keepdims=True))
        inv = lax.rsqrt(ss * (1.0 / MLA_QK_DIM) + RMS_EPS)
        rn = r * inv * g2
        o_ref[:, lo:lo + LANES] = n * inv * gn
        o_ref[:, lo + LANES:lo + MLA_QK_PAD] = rn * t1 + pltpu.roll(rn, MLA_ROPE_DIM, axis=1) * t2


def _mla_q_prep(qraw, t1, t2, g):
    T, W = qraw.shape
    tm = _tile(T, 256)
    row = lambda i: (i, 0)
    fix = lambda i: (0, 0)
    return pl.pallas_call(
        _mla_q_kernel,
        out_shape=jax.ShapeDtypeStruct((T, W), F32),
        grid=(T // tm,),
        in_specs=[pl.BlockSpec((tm, W), row), pl.BlockSpec((tm, LANES), row), pl.BlockSpec((tm, LANES), row),
                  pl.BlockSpec((1, LANES), fix), pl.BlockSpec((1, LANES), fix)],
        out_specs=pl.BlockSpec((tm, W), row),
        compiler_params=_params("parallel"),
        name="mla_q_prep",
    )(qraw, t1, t2, g[:MLA_NOPE_DIM].reshape(1, LANES), _rope_dup(g[MLA_NOPE_DIM:]).reshape(1, LANES))


def _mla_kv_kernel(kv_ref, kpe_ref, t1_ref, t2_ref, gn_ref, g2_ref, k_ref, v_ref):
    t1, t2, gn, g2 = t1_ref[...], t2_ref[...], gn_ref[...], g2_ref[...]
    kpe = kpe_ref[...]
    first = lax.broadcasted_iota(jnp.int32, t1.shape, 1) < MLA_ROPE_DIM
    sspe = jnp.sum(jnp.where(first, kpe * kpe, 0.0), axis=-1, keepdims=True)
    kr = kpe * g2
    kr = kr * t1 + pltpu.roll(kr, MLA_ROPE_DIM, axis=1) * t2
    nk = MLA_HEADS * MLA_NOPE_DIM
    for h in range(MLA_HEADS):
        n = kv_ref[:, h * LANES:(h + 1) * LANES]
        inv = lax.rsqrt((jnp.sum(n * n, axis=-1, keepdims=True) + sspe) * (1.0 / MLA_QK_DIM) + RMS_EPS)
        lo = h * MLA_QK_PAD
        k_ref[:, lo:lo + LANES] = (n * inv * gn).astype(k_ref.dtype)
        k_ref[:, lo + LANES:lo + MLA_QK_PAD] = (kr * inv).astype(k_ref.dtype)
    v_ref[...] = kv_ref[:, nk:].astype(v_ref.dtype)


def _mla_kv_prep(kv, kpe2, t1, t2, g):
    T = kv.shape[0]
    tm = _tile(T, 256)
    row = lambda i: (i, 0)
    fix = lambda i: (0, 0)
    kw, vw = MLA_HEADS * MLA_QK_PAD, MLA_HEADS * MLA_V_DIM
    return pl.pallas_call(
        _mla_kv_kernel,
        out_shape=(jax.ShapeDtypeStruct((T, kw), BF16), jax.ShapeDtypeStruct((T, vw), BF16)),
        grid=(T // tm,),
        in_specs=[pl.BlockSpec((tm, kv.shape[1]), row), pl.BlockSpec((tm, LANES), row),
                  pl.BlockSpec((tm, LANES), row), pl.BlockSpec((tm, LANES), row),
                  pl.BlockSpec((1, LANES), fix), pl.BlockSpec((1, LANES), fix)],
        out_specs=(pl.BlockSpec((tm, kw), row), pl.BlockSpec((tm, vw), row)),
        compiler_params=_params("parallel"),
        name="mla_kv_prep",
    )(kv, kpe2, t1, t2, g[:MLA_NOPE_DIM].reshape(1, LANES), _rope_dup(g[MLA_NOPE_DIM:]).reshape(1, LANES))


def _attn_kernel(q_ref, k_ref, v_ref, o_ref, *, scale, causal):
    q = q_ref[0].astype(BF16)
    k = k_ref[0].astype(BF16)
    lg = _nt(q, k) * scale
    if causal:
        tq = q.shape[0]
        qpos = pl.program_id(2) * tq + lax.broadcasted_iota(jnp.int32, lg.shape, 0)
        kpos = lax.broadcasted_iota(jnp.int32, lg.shape, 1)
        lg = jnp.where(kpos <= qpos, lg, -jnp.inf)
    m = jnp.max(lg, axis=-1, keepdims=True)
    p = jnp.exp(lg - m)
    l = jnp.sum(p, axis=-1, keepdims=True)
    o = jnp.dot(p.astype(BF16), v_ref[0].astype(BF16), preferred_element_type=F32)
    o_ref[0] = (o / l).astype(o_ref.dtype)


def _attn(q, k, v, *, heads, dk, dv, scale, causal, out_dtype=BF16, tq=256):
    B, S, _ = q.shape
    Sk = k.shape[1]
    tq = _tile(S, tq)
    return pl.pallas_call(
        functools.partial(_attn_kernel, scale=scale, causal=causal),
        out_shape=jax.ShapeDtypeStruct((B, S, heads * dv), out_dtype),
        grid=(B, heads, S // tq),
        in_specs=[pl.BlockSpec((1, tq, dk), lambda b, h, i: (b, i, h)),
                  pl.BlockSpec((1, Sk, dk), lambda b, h, i: (b, 0, h)),
                  pl.BlockSpec((1, Sk, dv), lambda b, h, i: (b, 0, h))],
        out_specs=pl.BlockSpec((1, tq, dv), lambda b, h, i: (b, i, h)),
        compiler_params=_params("parallel", "parallel", "parallel"),
        name="attn",
    )(q, k, v)


def _alibi_slopes(n):
    def pow2(m):
        start = 2.0 ** (-8.0 / m)
        return [start ** (i + 1) for i in range(m)]
    p = 2 ** int(math.floor(math.log2(n)))
    s = pow2(p)
    if p < n:
        s = s + pow2(2 * p)[0::2][: n - p]
    return np.array(s, dtype=np.float32)


def _top_blocks(gate, lane, k):
    width = gate.shape[1]
    picks = []
    for _ in range(k):
        m = jnp.max(gate, axis=-1, keepdims=True)
        idx = jnp.min(jnp.where(gate == m, lane, width), axis=-1, keepdims=True)
        idx = jnp.where(m > -jnp.inf, idx, -1)
        picks.append(idx)
        gate = jnp.where(lane == idx, -jnp.inf, gate)
    return picks


def _moba_prompt_kernel(q_ref, k_ref, v_ref, slope_ref, o_ref, *, scale):
    qf = q_ref[0]
    kf = k_ref[0]
    tq, S = qf.shape[0], kf.shape[0]
    nblk = S // MOBA_BLOCK
    kmean = jnp.mean(kf.reshape(nblk, MOBA_BLOCK, MOBA_HEAD_DIM), axis=1)
    kmean = jnp.concatenate([kmean, jnp.zeros((LANES - nblk, MOBA_HEAD_DIM), F32)], axis=0)
    gate = _nt(qf, kmean, HI)
    q0 = pl.program_id(2) * tq
    own = (q0 + lax.broadcasted_iota(jnp.int32, (tq, 1), 0)) // MOBA_BLOCK
    lane = lax.broadcasted_iota(jnp.int32, gate.shape, 1)
    gate = jnp.where(lane < own, gate, -jnp.inf)
    picks = _top_blocks(gate, lane, MOBA_TOPK)

    lg = _nt(qf.astype(BF16), kf.astype(BF16)) * scale
    qpos = q0 + lax.broadcasted_iota(jnp.int32, lg.shape, 0)
    kpos = lax.broadcasted_iota(jnp.int32, lg.shape, 1)
    slope = slope_ref[0, 0:1, 0:1]
    lg = lg - slope * (qpos - kpos).astype(F32)
    kblk = kpos // MOBA_BLOCK
    allowed = (kblk == own) & (kpos <= qpos)
    for idx in picks:
        allowed = allowed | (kblk == idx)
    lg = jnp.where(allowed, lg, -jnp.inf)
    m = jnp.max(lg, axis=-1, keepdims=True)
    p = jnp.exp(lg - m)
    l = jnp.sum(p, axis=-1, keepdims=True)
    o = jnp.dot(p.astype(BF16), v_ref[0].astype(BF16), preferred_element_type=F32)
    o_ref[0] = (o / l).astype(o_ref.dtype)


def _moba_prompt(q, k, v, slopes):
    B, S, _ = q.shape
    assert S % MOBA_BLOCK == 0 and S // MOBA_BLOCK <= LANES
    tq = _tile(S, 256)
    grp = MOBA_HEADS // MOBA_KV_HEADS
    d = MOBA_HEAD_DIM
    return pl.pallas_call(
        functools.partial(_moba_prompt_kernel, scale=d ** -0.5),
        out_shape=jax.ShapeDtypeStruct((B, S, MOBA_HEADS * d), BF16),
        grid=(B, MOBA_HEADS, S // tq),
        in_specs=[pl.BlockSpec((1, tq, d), lambda b, h, i: (b, i, h)),
                  pl.BlockSpec((1, S, d), lambda b, h, i: (b, 0, h // grp)),
                  pl.BlockSpec((1, S, d), lambda b, h, i: (b, 0, h // grp)),
                  pl.BlockSpec((1, 8, LANES), lambda b, h, i: (h, 0, 0))],
        out_specs=pl.BlockSpec((1, tq, d), lambda b, h, i: (b, i, h)),
        compiler_params=_params("parallel", "parallel", "parallel"),
        name="moba_prompt",
    )(q, k, v, jnp.broadcast_to(jnp.asarray(slopes)[:, None, None], (MOBA_HEADS, 8, LANES)))


def _page_copies(pool, pt_ref, b, chunk, buf, slot, sem, ppc, page):
    return [pltpu.make_async_copy(pool.at[pt_ref[b, chunk * ppc + p]],
                                  buf.at[slot, pl.ds(p * page, page)], sem.at[slot])
            for p in range(ppc)]


def _moba_sample_kernel(pt_ref, q_ref, slope_ref, qidx_ref, kn_ref, vn_ref, kpool, vpool, o_ref,
                        buf, kb, ksum, sem, *, n_pages, ppc, page, n_new, scale):
    b = pl.program_id(0)
    nch = n_pages // ppc
    ctok = ppc * page
    cblk = ctok // MOBA_BLOCK
    past = n_pages * page
    nb = past // MOBA_BLOCK
    d = MOBA_HEAD_DIM
    copies = functools.partial(_page_copies, pt_ref=pt_ref, b=b, buf=buf, sem=sem, ppc=ppc, page=page)

    def stream(pool, compute, carry):
        for c in copies(pool, chunk=0, slot=0):
            c.start()

        def body(c, carry):
            slot = c % 2
            for cp in copies(pool, chunk=c, slot=slot):
                cp.wait()

            @pl.when(c + 1 < nch)
            def _():
                for cp in copies(pool, chunk=c + 1, slot=1 - slot):
                    cp.start()
            return compute(c, buf[slot], carry)
        return lax.fori_loop(0, nch, body, carry)

    def keys(c, x, carry):
        ksum[c] = jnp.sum(x.reshape(cblk, MOBA_BLOCK, x.shape[1]), axis=1)
        kb[pl.ds(pl.multiple_of(c * ctok, ctok), ctok), :] = x.astype(BF16)
        return carry
    stream(kpool, keys, 0)

    kmean = ksum[...].reshape(nb, ksum.shape[2]) * (1.0 / MOBA_BLOCK)
    kmean = jnp.concatenate([kmean, jnp.zeros((LANES - nb, kmean.shape[1]), F32)], axis=0)
    rows = q_ref.shape[2]
    lane = lax.broadcasted_iota(jnp.int32, (rows, LANES), 1)
    qidx = qidx_ref[:, 0:1]
    qs, slopes, picks = [], [], []
    for g in range(MOBA_KV_HEADS):
        qf = q_ref[0, g]
        gate = _nt(qf, kmean[:, g * d:(g + 1) * d], HI)
        gate = jnp.where(lane < nb, gate, -jnp.inf)
        picks.append(_top_blocks(gate, lane, MOBA_TOPK))
        qs.append(qf)
        slopes.append(slope_ref[g][:, 0:1])

    def values(c, x, carry):
        kpos = c * ctok + lax.broadcasted_iota(jnp.int32, (rows, ctok), 1)
        kblk = kpos // MOBA_BLOCK
        xb = x.astype(BF16)
        out = []
        for g in range(MOBA_KV_HEADS):
            m, l, acc = carry[g]
            kc = kb[pl.ds(pl.multiple_of(c * ctok, ctok), ctok), g * d:(g + 1) * d]
            lg = _nt(qs[g].astype(BF16), kc) * scale
            lg = lg - slopes[g] * (past + qidx - kpos).astype(F32)
            allowed = (kblk == picks[g][0]) | (kblk == picks[g][1]) | (kblk == picks[g][2])
            m_new = jnp.maximum(m, jnp.max(jnp.where(allowed, lg, NEG), axis=-1, keepdims=True))
            p = jnp.where(allowed, jnp.exp(lg - m_new), 0.0)
            alpha = jnp.exp(m - m_new)
            l = l * alpha + jnp.sum(p, axis=-1, keepdims=True)
            acc = acc * alpha + jnp.dot(p.astype(BF16), xb[:, g * d:(g + 1) * d], preferred_element_type=F32)
            out.append((m_new, l, acc))
        return tuple(out)
    init = tuple((jnp.full((rows, 1), NEG, F32), jnp.zeros((rows, 1), F32), jnp.zeros((rows, d), F32))
                 for _ in range(MOBA_KV_HEADS))
    state = stream(vpool, values, init)

    for g in range(MOBA_KV_HEADS):
        m, l, acc = state[g]
        for j in range(n_new):
            kj = kn_ref[0, j:j + 1, g * d:(g + 1) * d]
            vj = vn_ref[0, j:j + 1, g * d:(g + 1) * d]
            lg = jnp.sum(qs[g] * kj, axis=-1, keepdims=True) * scale - slopes[g] * (qidx - j).astype(F32)
            ok = qidx >= j
            m_new = jnp.maximum(m, jnp.where(ok, lg, NEG))
            p = jnp.where(ok, jnp.exp(lg - m_new), 0.0)
            alpha = jnp.exp(m - m_new)
            l = l * alpha + p
            acc = acc * alpha + p * vj
            m = m_new
        o_ref[0, g] = acc / l


def _moba_sample(q, k_new, v_new, k_pool, v_pool, page_table, slopes):
    DB, Q, _ = q.shape
    n_phys, page = k_pool.shape[:2]
    n_pages = page_table.shape[1]
    past = n_pages * page
    assert past % MOBA_BLOCK == 0 and MOBA_TOPK <= past // MOBA_BLOCK <= LANES
    d, grp = MOBA_HEAD_DIM, MOBA_HEADS // MOBA_KV_HEADS
    kvw = MOBA_KV_HEADS * d
    ppc = _tile(n_pages, 16, 2 * MOBA_BLOCK // page)
    rows = Q * grp
    qg = q.reshape(DB, Q, MOBA_KV_HEADS, grp, d).transpose(0, 2, 1, 3, 4).reshape(DB, MOBA_KV_HEADS, rows, d)
    slope_rows = np.broadcast_to(np.tile(slopes.reshape(MOBA_KV_HEADS, 1, grp), (1, Q, 1)).reshape(
        MOBA_KV_HEADS, rows, 1), (MOBA_KV_HEADS, rows, LANES))
    qidx_rows = np.broadcast_to(np.repeat(np.arange(Q, dtype=np.int32), grp)[:, None], (rows, LANES))
    fix2 = lambda b, pt: (0, 0)
    out = pl.pallas_call(
        functools.partial(_moba_sample_kernel, n_pages=n_pages, ppc=ppc, page=page, n_new=Q, scale=d ** -0.5),
        out_shape=jax.ShapeDtypeStruct((DB, MOBA_KV_HEADS, rows, d), F32),
        grid_spec=pltpu.PrefetchScalarGridSpec(
            num_scalar_prefetch=1,
            grid=(DB,),
            in_specs=[pl.BlockSpec((1, MOBA_KV_HEADS, rows, d), lambda b, pt: (b, 0, 0, 0)),
                      pl.BlockSpec((MOBA_KV_HEADS, rows, LANES), lambda b, pt: (0, 0, 0)),
                      pl.BlockSpec((rows, LANES), fix2),
                      pl.BlockSpec((1, Q, kvw), lambda b, pt: (b, 0, 0)),
                      pl.BlockSpec((1, Q, kvw), lambda b, pt: (b, 0, 0)),
                      pl.BlockSpec(memory_space=pl.ANY),
                      pl.BlockSpec(memory_space=pl.ANY)],
            out_specs=pl.BlockSpec((1, MOBA_KV_HEADS, rows, d), lambda b, pt: (b, 0, 0, 0)),
            scratch_shapes=[pltpu.VMEM((2, ppc * page, kvw), F32),
                            pltpu.VMEM((past, kvw), BF16),
                            pltpu.VMEM((n_pages // ppc, ppc * page // MOBA_BLOCK, kvw), F32),
                            pltpu.SemaphoreType.DMA((2,))]),
        compiler_params=_params("arbitrary"),
        name="moba_sample",
    )(page_table, qg, jnp.asarray(slope_rows), jnp.asarray(qidx_rows), k_new, v_new,
      k_pool.reshape(n_phys, page, kvw), v_pool.reshape(n_phys, page, kvw))
    return out.reshape(DB, MOBA_KV_HEADS, Q, grp, d).transpose(0, 2, 1, 3, 4).reshape(DB, Q, MOBA_HEADS * d)


def _mla_sample_kernel(pt_ref, qw_ref, latn_ref, cc_ref, ss_ref, ccn_ref, ssn_ref, qidx_ref, wuk_ref,
                       sh_ref, sk_ref, pool, o_ref, buf, sem, *, n_pages, ppc, page, n_new, scale):
    b = pl.program_id(0)
    nch = n_pages // ppc
    ctok = ppc * page
    copies = functools.partial(_page_copies, pool, pt_ref, b, buf=buf, sem=sem, ppc=ppc, page=page)
    qw = qw_ref[0]
    r = MLA_KV_RANK
    q_abs, q_cos, q_sin = qw[:, :r], qw[:, r:r + MLA_ROPE_DIM], qw[:, r + MLA_ROPE_DIM:]
    wuk, sh, sk = wuk_ref[...], sh_ref[...], sk_ref[...]

    def attend(lat, cc, ss, allowed, carry):
        m, l, acc = carry
        c, kpe = lat[:, :r], lat[:, r:]
        cb = c.astype(BF16)
        kn = jnp.dot(cb, wuk, preferred_element_type=F32)
        ssq = _nt(sh, (kn * kn).astype(BF16)) + _nt(sk, (kpe * kpe).astype(BF16))
        inv = lax.rsqrt(ssq * (1.0 / MLA_QK_DIM) + RMS_EPS)
        raw = _nt(q_abs, cb) + _nt(q_cos, (kpe * cc).astype(BF16)) + _nt(q_sin, (kpe * ss).astype(BF16))
        lg = raw * inv * scale
        if allowed is not None:
            lg = jnp.where(allowed, lg, NEG)
        m_new = jnp.maximum(m, jnp.max(lg, axis=-1, keepdims=True))
        p = jnp.exp(lg - m_new)
        if allowed is not None:
            p = jnp.where(allowed, p, 0.0)
        alpha = jnp.exp(m - m_new)
        l = l * alpha + jnp.sum(p, axis=-1, keepdims=True)
        acc = acc * alpha + jnp.dot(p.astype(BF16), cb, preferred_element_type=F32)
        return m_new, l, acc

    for cp in copies(chunk=0, slot=0):
        cp.start()

    def body(c, carry):
        slot = c % 2
        for cp in copies(chunk=c, slot=slot):
            cp.wait()

        @pl.when(c + 1 < nch)
        def _():
            for cp in copies(chunk=c + 1, slot=1 - slot):
                cp.start()
        off = pl.multiple_of(c * ctok, ctok)
        return attend(buf[slot], cc_ref[pl.ds(off, ctok), :], ss_ref[pl.ds(off, ctok), :], None, carry)

    rows = qw.shape[0]
    init = (jnp.full((rows, 1), NEG, F32), jnp.zeros((rows, 1), F32), jnp.zeros((rows, r), F32))
    state = lax.fori_loop(0, nch, body, init)
    tn = latn_ref.shape[1]
    j = lax.broadcasted_iota(jnp.int32, (rows, tn), 1)
    allowed = (j <= qidx_ref[:, 0:1]) & (j < n_new)
    m, l, acc = attend(latn_ref[0], ccn_ref[...], ssn_ref[...], allowed, state)
    o_ref[0] = acc / l


def _rope_cs(pos):
    half = MLA_ROPE_DIM // 2
    inv = ROPE_THETA ** (-jnp.arange(half, dtype=F32) / half)
    ang = pos.astype(F32)[:, None] * inv
    cos, sin = jnp.cos(ang), jnp.sin(ang)
    return jnp.concatenate([cos, cos], axis=-1), jnp.concatenate([sin, sin], axis=-1)


def _abs_q_kernel(q_ref, g_ref, w_ref, o_ref):
    qg = (q_ref[...] * g_ref[...]).astype(BF16)
    o_ref[0] = _nt(qg, w_ref[...]).astype(o_ref.dtype)


def _head_out_kernel(a_ref, w_ref, o_ref):
    o_ref[...] = jnp.dot(a_ref[0], w_ref[...], preferred_element_type=F32).astype(o_ref.dtype)


def _mla_sample(qcat, lat_new, latent_pool, page_table, wuk_b, wuv_b, gk):
    DB, Q, _ = qcat.shape
    n_phys, page, latw = latent_pool.shape
    n_pages = page_table.shape[1]
    past = n_pages * page
    H, r, nd, rd = MLA_HEADS, MLA_KV_RANK, MLA_NOPE_DIM, MLA_ROPE_DIM
    M = DB * Q
    rows = LANES
    assert H * Q <= rows
    q4 = qcat.reshape(M, H, MLA_QK_PAD)
    q_abs = pl.pallas_call(
        _abs_q_kernel,
        out_shape=jax.ShapeDtypeStruct((H, M, r), BF16),
        grid=(H,),
        in_specs=[pl.BlockSpec((M, nd), lambda h: (0, 2 * h)), pl.BlockSpec((1, nd), lambda h: (0, 0)),
                  pl.BlockSpec((r, nd), lambda h: (0, h))],
        out_specs=pl.BlockSpec((1, M, r), lambda h: (h, 0, 0)),
        compiler_params=_params("parallel"),
        name="mla_abs_q",
    )(qcat.reshape(M, H * MLA_QK_PAD), gk[:nd].reshape(1, nd), wuk_b)
    qr = q4[:, :, nd:nd + rd]
    gr = gk[nd:]
    q_cos = qr * gr
    q_sin = jnp.concatenate([qr[..., rd // 2:], -qr[..., :rd // 2]], axis=-1) * gr
    qw = jnp.concatenate([q_abs.transpose(1, 0, 2), q_cos.astype(BF16), q_sin.astype(BF16)], axis=-1)
    qw = qw.reshape(DB, Q, H, r + 2 * rd).transpose(0, 2, 1, 3).reshape(DB, H * Q, r + 2 * rd)
    qw = jnp.pad(qw, ((0, 0), (0, rows - H * Q), (0, 0)))
    tn = LANES
    latn = jnp.pad(lat_new, ((0, 0), (0, tn - Q), (0, 0)))
    cc, ss = _rope_cs(jnp.arange(past))
    ccn, ssn = _rope_cs(past + jnp.arange(tn))
    qidx = np.zeros((rows, LANES), np.int32)
    qidx[:H * Q] = np.tile(np.arange(Q, dtype=np.int32), H)[:, None]
    sh = np.zeros((rows, H * nd), np.float32)
    for h in range(H):
        sh[h * Q:(h + 1) * Q, h * nd:(h + 1) * nd] = 1.0
    sk = np.zeros((rows, rd), np.float32)
    sk[:H * Q] = 1.0
    ppc = _tile(n_pages, 8, 1)
    fix2 = lambda b, pt: (0, 0)
    acc = pl.pallas_call(
        functools.partial(_mla_sample_kernel, n_pages=n_pages, ppc=ppc, page=page, n_new=Q, scale=MLA_QK_DIM ** -0.5),
        out_shape=jax.ShapeDtypeStruct((DB, rows, r), F32),
        grid_spec=pltpu.PrefetchScalarGridSpec(
            num_scalar_prefetch=1,
            grid=(DB,),
            in_specs=[pl.BlockSpec((1, rows, r + 2 * rd), lambda b, pt: (b, 0, 0)),
                      pl.BlockSpec((1, tn, latw), lambda b, pt: (b, 0, 0)),
                      pl.BlockSpec((past, rd), fix2), pl.BlockSpec((past, rd), fix2),
                      pl.BlockSpec((tn, rd), fix2), pl.BlockSpec((tn, rd), fix2),
                      pl.BlockSpec((rows, LANES), fix2),
                      pl.BlockSpec((r, H * nd), fix2),
                      pl.BlockSpec((rows, H * nd), fix2), pl.BlockSpec((rows, rd), fix2),
                      pl.BlockSpec(memory_space=pl.ANY)],
            out_specs=pl.BlockSpec((1, rows, r), lambda b, pt: (b, 0, 0)),
            scratch_shapes=[pltpu.VMEM((2, ppc * page, latw), F32), pltpu.SemaphoreType.DMA((2,))]),
        compiler_params=_params("arbitrary"),
        name="mla_sample",
    )(page_table, qw, latn, cc, ss, ccn, ssn, jnp.asarray(qidx), wuk_b,
      jnp.asarray(sh, BF16), jnp.asarray(sk, BF16), latent_pool)
    a = acc[:, :H * Q].reshape(DB, H, Q, r).transpose(1, 0, 2, 3).reshape(H, M, r).astype(BF16)
    return pl.pallas_call(
        _head_out_kernel,
        out_shape=jax.ShapeDtypeStruct((M, H * MLA_V_DIM), BF16),
        grid=(H,),
        in_specs=[pl.BlockSpec((1, M, r), lambda h: (h, 0, 0)), pl.BlockSpec((r, MLA_V_DIM), lambda h: (0, h))],
        out_specs=pl.BlockSpec((M, MLA_V_DIM), lambda h: (0, h)),
        compiler_params=_params("parallel"),
        name="mla_head_out",
    )(a, wuv_b)


def _merge_kernel(oa_ref, ob_ref, om_ref, wa_ref, wb_ref, wm_ref, ga_ref, gb_ref, gm_ref, o_ref):
    def branch(o, w, g):
        return jax.nn.sigmoid(g[...]) * jnp.dot(o[...], w[...], preferred_element_type=F32)
    o_ref[...] = (branch(oa_ref, wa_ref, ga_ref) + branch(ob_ref, wb_ref, gb_ref)
                  + branch(om_ref, wm_ref, gm_ref)).astype(o_ref.dtype)


def _merge(oa, ob, om, wa, wb, wm, gates):
    T, D = oa.shape[0], wa.shape[1]
    tm, tn = _tile(T, 512), _tile(D, 512, LANES)
    nj = D // tn
    row = lambda i, j: (i, 0)
    col = lambda i, j: (0, j)
    return pl.pallas_call(
        _merge_kernel,
        out_shape=jax.ShapeDtypeStruct((T, D), BF16),
        grid=(T // tm, nj),
        in_specs=[pl.BlockSpec((tm, oa.shape[1]), row), pl.BlockSpec((tm, ob.shape[1]), row),
                  pl.BlockSpec((tm, om.shape[1]), row),
                  pl.BlockSpec((wa.shape[0], tn), col), pl.BlockSpec((wb.shape[0], tn), col),
                  pl.BlockSpec((wm.shape[0], tn), col),
                  pl.BlockSpec((tm, tn), lambda i, j: (i, j)),
                  pl.BlockSpec((tm, tn), lambda i, j: (i, j + nj)),
                  pl.BlockSpec((tm, tn), lambda i, j: (i, j + 2 * nj))],
        out_specs=pl.BlockSpec((tm, tn), lambda i, j: (i, j)),
        compiler_params=_params("parallel", "parallel"),
        name="merge",
    )(oa, ob, om, wa, wb, wm, gates, gates, gates)


def _top_rows(s, row, k):
    n = s.shape[0]
    vals, idxs = [], []
    for _ in range(k):
        m = jnp.max(s, axis=0, keepdims=True)
        i = jnp.min(jnp.where(s == m, row, n), axis=0, keepdims=True)
        vals.append(m)
        idxs.append(i)
        s = jnp.where(row == i, -jnp.inf, s)
    return vals, idxs


def _peer_route_kernel(q_ref, sk_ref, e_ref, g_ref):
    nk, k = PEER_N_KEYS, PEER_TOPK
    tm = q_ref.shape[0]
    row = lax.broadcasted_iota(jnp.int32, (nk, tm), 0)
    sv, si = [], []
    for p in range(2):
        s = _nt(sk_ref[0, p], q_ref[:, p * LANES:(p + 1) * LANES], HI)
        vals, idxs = _top_rows(s, row, k)
        sv.append(vals)
        si.append(idxs)
    sv1 = jnp.concatenate(sv[1], axis=0)
    si1 = jnp.concatenate(si[1], axis=0)
    cand = jnp.concatenate([sv[0][a] + sv1 for a in range(k)], axis=0)
    cidx = jnp.concatenate([si[0][a] * nk + si1 for a in range(k)], axis=0)
    pos = lax.broadcasted_iota(jnp.int32, cand.shape, 0)
    tv, te = [], []
    for _ in range(k):
        m = jnp.max(cand, axis=0, keepdims=True)
        j = jnp.min(jnp.where(cand == m, pos, k * k), axis=0, keepdims=True)
        pick = pos == j
        te.append(jnp.max(jnp.where(pick, cidx, -1), axis=0, keepdims=True))
        tv.append(m)
        cand = jnp.where(pick, -jnp.inf, cand)
    ts = jnp.concatenate(tv, axis=0)
    ex = jnp.exp(ts - tv[0])
    e_ref[0] = jnp.concatenate(te, axis=0)
    g_ref[0] = ex / jnp.sum(ex, axis=0, keepdims=True)


def _peer_route(qp, subkeys):
    T = qp.shape[0]
    tm = _tile(T, 256, LANES) if T % LANES == 0 else T
    nt = T // tm
    slots = PEER_HEADS * PEER_TOPK
    e, g = pl.pallas_call(
        _peer_route_kernel,
        out_shape=(jax.ShapeDtypeStruct((nt, slots, tm), jnp.int32), jax.ShapeDtypeStruct((nt, slots, tm), F32)),
        grid=(nt, PEER_HEADS),
        in_specs=[pl.BlockSpec((tm, PEER_QUERY_DIM), lambda i, h: (i, h)),
                  pl.BlockSpec((1, 2, PEER_N_KEYS, PEER_QUERY_DIM // 2), lambda i, h: (h, 0, 0, 0))],
        out_specs=(pl.BlockSpec((1, PEER_TOPK, tm), lambda i, h: (i, h, 0)),
                   pl.BlockSpec((1, PEER_TOPK, tm), lambda i, h: (i, h, 0))),
        compiler_params=_params("parallel", "parallel"),
        name="peer_route",
    )(qp, subkeys)
    return e.transpose(0, 2, 1).reshape(T, slots), g.transpose(0, 2, 1).reshape(T, slots)


def _peer_kernel(h_ref, a_ref, b_ref, g_ref, u_ref, v_ref, x_ref, o_ref, wbuf, wtmp, *, grp):
    e = pl.program_id(1)
    tt = h_ref.shape[0]
    nk = PEER_N_KEYS
    te = u_ref.shape[0]

    @pl.when(e == 0)
    def _():
        o_ref[...] = x_ref[...]
        sub = lax.broadcasted_iota(jnp.int32, (nk, a_ref.shape[1]), 0)

        def build(t8, carry):
            t0 = pl.multiple_of(t8 * grp, grp)
            for kk in range(grp):
                a = a_ref[pl.ds(t0 + kk, 1), :]
                bb = b_ref[pl.ds(t0 + kk, 1), :]
                gg = g_ref[pl.ds(t0 + kk, 1), :]
                ahot = jnp.where(sub == a, 1.0, 0.0).astype(BF16)
                bw = jnp.where(sub == bb, gg, 0.0)
                bhi = bw.astype(BF16)
                blo = (bw - bhi.astype(F32)).astype(BF16)
                wtmp[kk] = _nt(ahot, bhi) + _nt(ahot, blo)
            for i in range(nk):
                wbuf[i, pl.ds(t0, grp), :] = wtmp[:, i, :]
            return carry
        lax.fori_loop(0, tt // grp, build, 0)

    z = _nt(h_ref[...], u_ref[...])
    act = 0.5 * z * (1.0 + lax.erf(z * math.sqrt(0.5)))
    w = jnp.concatenate([wbuf[e * (te // nk) + r] for r in range(te // nk)], axis=1)
    o_ref[...] += jnp.dot((w * act).astype(BF16), v_ref[...], preferred_element_type=F32)


def _peer(hn, eidx, gate, u_b, v_b, x):
    T, D = x.shape
    E = u_b.shape[0]
    nk = PEER_N_KEYS
    tt = _tile(T, 256)
    te = 2 * nk
    grp = 8
    assert tt % grp == 0
    row = lambda i, e: (i, 0)
    exp = lambda i, e: (e, 0)
    slots = eidx.shape[1]
    return pl.pallas_call(
        functools.partial(_peer_kernel, grp=grp),
        out_shape=jax.ShapeDtypeStruct((T, D), F32),
        grid=(T // tt, E // te),
        in_specs=[pl.BlockSpec((tt, D), row), pl.BlockSpec((tt, slots), row), pl.BlockSpec((tt, slots), row),
                  pl.BlockSpec((tt, slots), row), pl.BlockSpec((te, D), exp), pl.BlockSpec((te, D), exp),
                  pl.BlockSpec((tt, D), row)],
        out_specs=pl.BlockSpec((tt, D), row),
        scratch_shapes=[pltpu.VMEM((nk, tt, nk), F32), pltpu.VMEM((grp, nk, nk), F32)],
        compiler_params=_params("parallel", "arbitrary"),
        name="peer",
    )(hn, eidx // nk, eidx % nk, gate, u_b, v_b, x)


def _layer(xp, xs, mem_prompt, c_moba_k, c_moba_v, c_mla_lat, c_mem_k, c_mem_v, page_table,
           norm_attn_g, norm_ffn_g, norm_mem_g, w_in, moba_q_norm_g, moba_k_norm_g,
           mla_cq_norm_g, mla_ckv_norm_g, w_mla_uq, w_mla_uk, w_mla_uv, mla_q_norm_g, mla_k_norm_g,
           w_mem_k, w_mem_v, mem_q_norm_g, mem_k_norm_g, w_moba_o, w_mla_o, w_mem_o, w_out,
           w_peer_q, peer_subkeys, peer_u, peer_v):
    B, S, D = xp.shape
    DB, Q, _ = xs.shape
    Tp, Ts = B * S, DB * Q
    n_pages = page_table.shape[1]
    past = n_pages * c_moba_k.shape[1]
    slopes = _alibi_slopes(MOBA_HEADS)
    qa_w, kv_w = MOBA_HEADS * MOBA_HEAD_DIM, MOBA_KV_HEADS * MOBA_HEAD_DIM
    qm_w = MEM_HEADS * MEM_HEAD_DIM
    cuts = np.cumsum([qa_w, kv_w, kv_w, MLA_Q_RANK, MLA_KV_RANK, MLA_ROPE_DIM, qm_w])
    c_qa, c_ka, c_va, c_cq, c_ckv, c_kpe, c_qm = [int(c) for c in cuts]

    x = jnp.concatenate([xp.reshape(Tp, D), xs.reshape(Ts, D)], axis=0)
    pos = jnp.concatenate([jnp.tile(jnp.arange(S), B), jnp.tile(past + jnp.arange(Q), DB)])
    t1, t2 = _rope_tables(pos)

    hn = _headnorm(x, norm_attn_g, D, BF16)
    w_attn = jnp.concatenate([w_in[:, :c_ckv], w_in[:, c_kpe:c_qm]], axis=1).astype(BF16)
    w_kpe2 = _rope_dup(w_in[:, c_ckv:c_kpe]).astype(BF16)
    pa = _mm(hn, w_attn)
    kpe2 = _mm(hn, w_kpe2)
    gates = _mm(hn, w_in[:, c_qm:].astype(BF16))
    qa = _headnorm(pa[:, :c_qa], moba_q_norm_g, MOBA_HEAD_DIM, F32)
    ka = _headnorm(pa[:, c_qa:c_ka], moba_k_norm_g, MOBA_HEAD_DIM, F32)
    va = pa[:, c_ka:c_va]
    cq = _headnorm(pa[:, c_va:c_cq], mla_cq_norm_g, MLA_Q_RANK, BF16)
    ckv = _headnorm(pa[:, c_cq:c_ckv], mla_ckv_norm_g, MLA_KV_RANK, F32)
    qm = _headnorm(pa[:, c_ckv:], mem_q_norm_g, MEM_HEAD_DIM, BF16)
    kpe = kpe2[:, :MLA_ROPE_DIM]
    lat = jnp.concatenate([ckv, kpe], axis=-1)

    h_idx = np.arange(MLA_HEADS)[:, None] * MLA_QK_DIM
    half = MLA_ROPE_DIM // 2
    r_idx = MLA_NOPE_DIM + np.concatenate([np.arange(MLA_ROPE_DIM), np.arange(half, MLA_ROPE_DIM), np.arange(half)])
    cols = (h_idx + np.concatenate([np.arange(MLA_NOPE_DIM), r_idx])[None, :]).reshape(-1)
    qcat = _mla_q_prep(_mm(cq, w_mla_uq[:, cols].astype(BF16)), t1, t2, mla_q_norm_g)
    wuk_b, wuv_b = w_mla_uk.astype(BF16), w_mla_uv.astype(BF16)

    o_a = _moba_prompt(qa[:Tp].reshape(B, S, qa_w), ka[:Tp].reshape(B, S, kv_w), va[:Tp].reshape(B, S, kv_w), slopes)
    kvp = _mm(ckv[:Tp].astype(BF16), jnp.concatenate([wuk_b, wuv_b], axis=1))
    kb, vb = _mla_kv_prep(kvp, kpe2[:Tp], t1[:Tp], t2[:Tp], mla_k_norm_g)
    o_b = _attn(qcat[:Tp].reshape(B, S, -1), kb.reshape(B, S, -1), vb.reshape(B, S, -1), heads=MLA_HEADS,
                dk=MLA_QK_PAD, dv=MLA_V_DIM, scale=MLA_QK_DIM ** -0.5, causal=True)
    M = mem_prompt.shape[1]
    mn = _headnorm(mem_prompt.reshape(B * M, D), norm_mem_g, D, BF16)
    mkv = _mm(mn, jnp.concatenate([w_mem_k, w_mem_v], axis=1).astype(BF16))
    mk = _headnorm(mkv[:, :qm_w], mem_k_norm_g, MEM_HEAD_DIM, F32)
    mv = mkv[:, qm_w:]
    mem_scale = MEM_HEAD_DIM ** -0.5
    o_m = _attn(qm[:Tp].reshape(B, S, qm_w), mk.reshape(B, M, qm_w), mv.reshape(B, M, qm_w), heads=MEM_HEADS,
                dk=MEM_HEAD_DIM, dv=MEM_HEAD_DIM, scale=mem_scale, causal=False)

    o_a_s = _moba_sample(qa[Tp:].reshape(DB, Q, qa_w), ka[Tp:].reshape(DB, Q, kv_w), va[Tp:].reshape(DB, Q, kv_w),
                         c_moba_k, c_moba_v, page_table, slopes)
    o_b_s = _mla_sample(qcat[Tp:].reshape(DB, Q, -1), lat[Tp:].reshape(DB, Q, -1), c_mla_lat, page_table,
                        wuk_b, wuv_b, mla_k_norm_g)
    qpad = 8
    qm_s = jnp.pad(qm[Tp:].reshape(DB, Q, qm_w), ((0, 0), (0, qpad - Q), (0, 0)))
    Ms = c_mem_k.shape[1]
    o_m_s = _attn(qm_s, c_mem_k.reshape(DB, Ms, qm_w), c_mem_v.reshape(DB, Ms, qm_w), heads=MEM_HEADS,
                  dk=MEM_HEAD_DIM, dv=MEM_HEAD_DIM, scale=mem_scale, causal=False)[:, :Q]

    oa = jnp.concatenate([o_a.reshape(Tp, -1), o_a_s.reshape(Ts, -1).astype(BF16)], axis=0)
    ob = jnp.concatenate([o_b.reshape(Tp, -1), o_b_s], axis=0)
    om = jnp.concatenate([o_m.reshape(Tp, -1), o_m_s.reshape(Ts, -1)], axis=0)
    mg = _merge(oa, ob, om, w_moba_o.astype(BF16), w_mla_o.astype(BF16), w_mem_o.astype(BF16), gates)
    x1 = _mm(mg, w_out.astype(BF16), residual=x)

    hf = _headnorm(x1, norm_ffn_g, D, BF16)
    eidx, gate = _peer_route(_mm(hf, w_peer_q.astype(BF16)), peer_subkeys)
    y = _peer(hf, eidx, gate, peer_u.astype(BF16), peer_v.astype(BF16), x1)

    kvh = (MOBA_KV_HEADS, MOBA_HEAD_DIM)
    states = (ka[:Tp].reshape(B, S, *kvh), va[:Tp].reshape(B, S, *kvh), lat[:Tp].reshape(B, S, -1),
              mk.reshape(B, M, MEM_HEADS, MEM_HEAD_DIM), mv.reshape(B, M, MEM_HEADS, MEM_HEAD_DIM),
              ka[Tp:].reshape(DB, Q, *kvh), va[Tp:].reshape(DB, Q, *kvh), lat[Tp:].reshape(DB, Q, -1))
    return y[:Tp].reshape(B, S, D), y[Tp:].reshape(DB, Q, D), states


def kernel(x_prompt, x_sample, mem_prompt, cache_moba_k, cache_moba_v, cache_mla_latent, cache_mem_k, cache_mem_v, page_table, norm_attn_g, norm_ffn_g, norm_mem_g, w_in, moba_q_norm_g, moba_k_norm_g, mla_cq_norm_g, mla_ckv_norm_g, w_mla_uq, w_mla_uk, w_mla_uv, mla_q_norm_g, mla_k_norm_g, w_mem_k, w_mem_v, mem_q_norm_g, mem_k_norm_g, w_moba_o, w_mla_o, w_mem_o, w_out, w_peer_q, peer_subkeys, peer_u, peer_v):
    xp, xs = x_prompt, x_sample
    new = [[] for _ in range(8)]
    for l in range(w_in.shape[0]):
        xp, xs, states = _layer(
            xp, xs, mem_prompt, cache_moba_k[l], cache_moba_v[l], cache_mla_latent[l], cache_mem_k[l],
            cache_mem_v[l], page_table, norm_attn_g[l], norm_ffn_g[l], norm_mem_g[l], w_in[l],
            moba_q_norm_g[l], moba_k_norm_g[l], mla_cq_norm_g[l], mla_ckv_norm_g[l], w_mla_uq[l], w_mla_uk[l],
            w_mla_uv[l], mla_q_norm_g[l], mla_k_norm_g[l], w_mem_k[l], w_mem_v[l], mem_q_norm_g[l],
            mem_k_norm_g[l], w_moba_o[l], w_mla_o[l], w_mem_o[l], w_out[l], w_peer_q[l], peer_subkeys[l],
            peer_u[l], peer_v[l])
        for lst, val in zip(new, states):
            lst.append(val)
    return (xp, xs) + tuple(jnp.stack(v) for v in new)
```

```python
import functools
import math

import numpy as np
import jax
import jax.numpy as jnp
from jax import lax
from jax.experimental import pallas as pl
from jax.experimental.pallas import tpu as pltpu

MOBA_HEADS = 12
MOBA_KV_HEADS = 2
MOBA_HEAD_DIM = 128
MOBA_BLOCK = 256
MOBA_TOPK = 3
MLA_HEADS = 12
MLA_Q_RANK = 768
MLA_KV_RANK = 512
MLA_NOPE_DIM = 128
MLA_ROPE_DIM = 64
MLA_QK_DIM = MLA_NOPE_DIM + MLA_ROPE_DIM
MLA_V_DIM = 128
ROPE_THETA = 10000.0
MEM_HEADS = 4
MEM_HEAD_DIM = 256
PEER_HEADS = 8
PEER_N_KEYS = 128
PEER_QUERY_DIM = 256
PEER_TOPK = 16
RMS_EPS = 1e-6

LANES = 128
MLA_QK_PAD = 2 * LANES
VMEM_LIMIT = 56 * 1024 * 1024
NEG = -1e30
HI = lax.Precision.HIGHEST
F32 = jnp.float32
BF16 = jnp.bfloat16
NT_DIMS = (((1,), (1,)), ((), ()))


def _tile(n, pref, align=8):
    t = (min(pref, n) // align) * align
    while t >= align:
        if n % t == 0:
            return t
        t -= align
    return n


def _params(*sem):
    return pltpu.CompilerParams(dimension_semantics=sem, vmem_limit_bytes=VMEM_LIMIT)


def _nt(a, b, precision=None):
    return lax.dot_general(a, b, NT_DIMS, precision=precision, preferred_element_type=F32)


def _headnorm_kernel(x_ref, g_ref, o_ref, *, hd):
    g = g_ref[...]
    for h in range(x_ref.shape[1] // hd):
        x = x_ref[:, h * hd:(h + 1) * hd]
        inv = lax.rsqrt(jnp.mean(x * x, axis=-1, keepdims=True) + RMS_EPS)
        o_ref[:, h * hd:(h + 1) * hd] = (x * inv * g).astype(o_ref.dtype)


def _headnorm(x, g, hd, out_dtype):
    T, W = x.shape
    tm = _tile(T, 256)
    return pl.pallas_call(
        functools.partial(_headnorm_kernel, hd=hd),
        out_shape=jax.ShapeDtypeStruct((T, W), out_dtype),
        grid=(T // tm,),
        in_specs=[pl.BlockSpec((tm, W), lambda i: (i, 0)), pl.BlockSpec((1, hd), lambda i: (0, 0))],
        out_specs=pl.BlockSpec((tm, W), lambda i: (i, 0)),
        compiler_params=_params("parallel"),
        name="headnorm",
    )(x, g.reshape(1, hd).astype(F32))


def _mm_kernel(a_ref, w_ref, o_ref):
    o_ref[...] = jnp.dot(a_ref[...], w_ref[...], preferred_element_type=F32).astype(o_ref.dtype)


def _mm_res_kernel(a_ref, w_ref, r_ref, o_ref):
    o_ref[...] = r_ref[...] + jnp.dot(a_ref[...], w_ref[...], preferred_element_type=F32)


def _mm(a, w, *, residual=None, tm=512, tn=512, out_dtype=F32):
    M, K = a.shape
    N = w.shape[1]
    tm, tn = _tile(M, tm), _tile(N, tn, LANES)
    in_specs = [pl.BlockSpec((tm, K), lambda i, j: (i, 0)), pl.BlockSpec((K, tn), lambda i, j: (0, j))]
    args = [a, w]
    body = _mm_kernel
    if residual is not None:
        in_specs.append(pl.BlockSpec((tm, tn), lambda i, j: (i, j)))
        args.append(residual)
        body = _mm_res_kernel
    return pl.pallas_call(
        body,
        out_shape=jax.ShapeDtypeStruct((M, N), out_dtype),
        grid=(M // tm, N // tn),
        in_specs=in_specs,
        out_specs=pl.BlockSpec((tm, tn), lambda i, j: (i, j)),
        compiler_params=_params("parallel", "parallel"),
        name="mm",
    )(*args)


def _rope_tables(pos):
    half = MLA_ROPE_DIM // 2
    inv = ROPE_THETA ** (-jnp.arange(half, dtype=F32) / half)
    ang = pos.astype(F32)[:, None] * inv
    cos, sin = jnp.cos(ang), jnp.sin(ang)
    z = jnp.zeros_like(cos)
    return jnp.concatenate([cos, cos, z, z], axis=-1), jnp.concatenate([-sin, sin, z, z], axis=-1)


def _rope_dup(v):
    half = MLA_ROPE_DIM // 2
    return jnp.concatenate([v, v[..., half:], v[..., :half]], axis=-1)


def _mla_q_kernel(q_ref, t1_ref, t2_ref, gn_ref, g2_ref, o_ref):
    t1, t2, gn, g2 = t1_ref[...], t2_ref[...], gn_ref[...], g2_ref[...]
    first = lax.broadcasted_iota(jnp.int32, t1.shape, 1) < MLA_ROPE_DIM
    for h in range(MLA_HEADS):
        lo = h * MLA_QK_PAD
        n = q_ref[:, lo:lo + LANES]
        r = q_ref[:, lo + LANES:lo + MLA_QK_PAD]
        ss = (jnp.sum(n * n, axis=-1, keepdims=True)
              + jnp.sum(jnp.where(first, r * r, 0.0), axis=-1, keepdims=True))
        inv = lax.rsqrt(ss * (1.0 / MLA_QK_DIM) + RMS_EPS)
        rn = r * inv * g2
        o_ref[:, lo:lo + LANES] = n * inv * gn
        o_ref[:, lo + LANES:lo + MLA_QK_PAD] = rn * t1 + pltpu.roll(rn, MLA_ROPE_DIM, axis=1) * t2


def _mla_q_prep(qraw, t1, t2, g):
    T, W = qraw.shape
    tm = _tile(T, 256)
    row = lambda i: (i, 0)
    fix = lambda i: (0, 0)
    return pl.pallas_call(
        _mla_q_kernel,
        out_shape=jax.ShapeDtypeStruct((T, W), F32),
        grid=(T // tm,),
        in_specs=[pl.BlockSpec((tm, W), row), pl.BlockSpec((tm, LANES), row), pl.BlockSpec((tm, LANES), row),
                  pl.BlockSpec((1, LANES), fix), pl.BlockSpec((1, LANES), fix)],
        out_specs=pl.BlockSpec((tm, W), row),
        compiler_params=_params("parallel"),
        name="mla_q_prep",
    )(qraw, t1, t2, g[:MLA_NOPE_DIM].reshape(1, LANES), _rope_dup(g[MLA_NOPE_DIM:]).reshape(1, LANES))


def _mla_kv_kernel(kv_ref, kpe_ref, t1_ref, t2_ref, gn_ref, g2_ref, k_ref, v_ref):
    t1, t2, gn, g2 = t1_ref[...], t2_ref[...], gn_ref[...], g2_ref[...]
    kpe = kpe_ref[...]
    first = lax.broadcasted_iota(jnp.int32, t1.shape, 1) < MLA_ROPE_DIM
    sspe = jnp.sum(jnp.where(first, kpe * kpe, 0.0), axis=-1, keepdims=True)
    kr = kpe * g2
    kr = kr * t1 + pltpu.roll(kr, MLA_ROPE_DIM, axis=1) * t2
    nk = MLA_HEADS * MLA_NOPE_DIM
    for h in range(MLA_HEADS):
        n = kv_ref[:, h * LANES:(h + 1) * LANES]
        inv = lax.rsqrt((jnp.sum(n * n, axis=-1, keepdims=True) + sspe) * (1.0 / MLA_QK_DIM) + RMS_EPS)
        lo = h * MLA_QK_PAD
        k_ref[:, lo:lo + LANES] = (n * inv * gn).astype(k_ref.dtype)
        k_ref[:, lo + LANES:lo + MLA_QK_PAD] = (kr * inv).astype(k_ref.dtype)
    v_ref[...] = kv_ref[:, nk:].astype(v_ref.dtype)


def _mla_kv_prep(kv, kpe2, t1, t2, g):
    T = kv.shape[0]
    tm = _tile(T, 256)
    row = lambda i: (i, 0)
    fix = lambda i: (0, 0)
    kw, vw = MLA_HEADS * MLA_QK_PAD, MLA_HEADS * MLA_V_DIM
    return pl.pallas_call(
        _mla_kv_kernel,
        out_shape=(jax.ShapeDtypeStruct((T, kw), BF16), jax.ShapeDtypeStruct((T, vw), BF16)),
        grid=(T // tm,),
        in_specs=[pl.BlockSpec((tm, kv.shape[1]), row), pl.BlockSpec((tm, LANES), row),
                  pl.BlockSpec((tm, LANES), row), pl.BlockSpec((tm, LANES), row),
                  pl.BlockSpec((1, LANES), fix), pl.BlockSpec((1, LANES), fix)],
        out_specs=(pl.BlockSpec((tm, kw), row), pl.BlockSpec((tm, vw), row)),
        compiler_params=_params("parallel"),
        name="mla_kv_prep",
    )(kv, kpe2, t1, t2, g[:MLA_NOPE_DIM].reshape(1, LANES), _rope_dup(g[MLA_NOPE_DIM:]).reshape(1, LANES))


def _attn_kernel(q_ref, k_ref, v_ref, o_ref, *, scale, causal):
    q = q_ref[0].astype(BF16)
    k = k_ref[0].astype(BF16)
    lg = _nt(q, k) * scale
    if causal:
        tq = q.shape[0]
        qpos = pl.program_id(2) * tq + lax.broadcasted_iota(jnp.int32, lg.shape, 0)
        kpos = lax.broadcasted_iota(jnp.int32, lg.shape, 1)
        lg = jnp.where(kpos <= qpos, lg, -jnp.inf)
    m = jnp.max(lg, axis=-1, keepdims=True)
    p = jnp.exp(lg - m)
    l = jnp.sum(p, axis=-1, keepdims=True)
    o = jnp.dot(p.astype(BF16), v_ref[0].astype(BF16), preferred_element_type=F32)
    o_ref[0] = (o / l).astype(o_ref.dtype)


def _attn(q, k, v, *, heads, dk, dv, scale, causal, out_dtype=BF16, tq=256):
    B, S, _ = q.shape
    Sk = k.shape[1]
    tq = _tile(S, tq)
    return pl.pallas_call(
        functools.partial(_attn_kernel, scale=scale, causal=causal),
        out_shape=jax.ShapeDtypeStruct((B, S, heads * dv), out_dtype),
        grid=(B, heads, S // tq),
        in_specs=[pl.BlockSpec((1, tq, dk), lambda b, h, i: (b, i, h)),
                  pl.BlockSpec((1, Sk, dk), lambda b, h, i: (b, 0, h)),
                  pl.BlockSpec((1, Sk, dv), lambda b, h, i: (b, 0, h))],
        out_specs=pl.BlockSpec((1, tq, dv), lambda b, h, i: (b, i, h)),
        compiler_params=_params("parallel", "parallel", "parallel"),
        name="attn",
    )(q, k, v)


def _alibi_slopes(n):
    def pow2(m):
        start = 2.0 ** (-8.0 / m)
        return [start ** (i + 1) for i in range(m)]
    p = 2 ** int(math.floor(math.log2(n)))
    s = pow2(p)
    if p < n:
        s = s + pow2(2 * p)[0::2][: n - p]
    return np.array(s, dtype=np.float32)


def _top_blocks(gate, lane, k):
    width = gate.shape[1]
    picks = []
    for _ in range(k):
        m = jnp.max(gate, axis=-1, keepdims=True)
        idx = jnp.min(jnp.where(gate == m, lane, width), axis=-1, keepdims=True)
        idx = jnp.where(m > -jnp.inf, idx, -1)
        picks.append(idx)
        gate = jnp.where(lane == idx, -jnp.inf, gate)
    return picks


def _moba_prompt_kernel(q_ref, k_ref, v_ref, slope_ref, o_ref, *, scale):
    qf = q_ref[0]
    kf = k_ref[0]
    tq, S = qf.shape[0], kf.shape[0]
    nblk = S // MOBA_BLOCK
    kmean = jnp.mean(kf.reshape(nblk, MOBA_BLOCK, MOBA_HEAD_DIM), axis=1)
    kmean = jnp.concatenate([kmean, jnp.zeros((LANES - nblk, MOBA_HEAD_DIM), F32)], axis=0)
    gate = _nt(qf, kmean, HI)
    q0 = pl.program_id(2) * tq
    own = (q0 + lax.broadcasted_iota(jnp.int32, (tq, 1), 0)) // MOBA_BLOCK
    lane = lax.broadcasted_iota(jnp.int32, gate.shape, 1)
    gate = jnp.where(lane < own, gate, -jnp.inf)
    picks = _top_blocks(gate, lane, MOBA_TOPK)

    lg = _nt(qf.astype(BF16), kf.astype(BF16)) * scale
    qpos = q0 + lax.broadcasted_iota(jnp.int32, lg.shape, 0)
    kpos = lax.broadcasted_iota(jnp.int32, lg.shape, 1)
    slope = slope_ref[0, 0:1, 0:1]
    lg = lg - slope * (qpos - kpos).astype(F32)
    kblk = kpos // MOBA_BLOCK
    allowed = (kblk == own) & (kpos <= qpos)
    for idx in picks:
        allowed = allowed | (kblk == idx)
    lg = jnp.where(allowed, lg, -jnp.inf)
    m = jnp.max(lg, axis=-1, keepdims=True)
    p = jnp.exp(lg - m)
    l = jnp.sum(p, axis=-1, keepdims=True)
    o = jnp.dot(p.astype(BF16), v_ref[0].astype(BF16), preferred_element_type=F32)
    o_ref[0] = (o / l).astype(o_ref.dtype)


def _moba_prompt(q, k, v, slopes):
    B, S, _ = q.shape
    assert S % MOBA_BLOCK == 0 and S // MOBA_BLOCK <= LANES
    tq = _tile(S, 256)
    grp = MOBA_HEADS // MOBA_KV_HEADS
    d = MOBA_HEAD_DIM
    return pl.pallas_call(
        functools.partial(_moba_prompt_kernel, scale=d ** -0.5),
        out_shape=jax.ShapeDtypeStruct((B, S, MOBA_HEADS * d), BF16),
        grid=(B, MOBA_HEADS, S // tq),
        in_specs=[pl.BlockSpec((1, tq, d), lambda b, h, i: (b, i, h)),
                  pl.BlockSpec((1, S, d), lambda b, h, i: (b, 0, h // grp)),
                  pl.BlockSpec((1, S, d), lambda b, h, i: (b, 0, h // grp)),
                  pl.BlockSpec((1, 8, LANES), lambda b, h, i: (h, 0, 0))],
        out_specs=pl.BlockSpec((1, tq, d), lambda b, h, i: (b, i, h)),
        compiler_params=_params("parallel", "parallel", "parallel"),
        name="moba_prompt",
    )(q, k, v, jnp.broadcast_to(jnp.asarray(slopes)[:, None, None], (MOBA_HEADS, 8, LANES)))


def _page_copies(pool, pt_ref, b, base, chunk, buf, slot, sem, ppc, rows):
    out = []
    for p in range(ppc):
        row0 = pl.multiple_of((base + pt_ref[b, chunk * ppc + p]) * rows, rows)
        out.append(pltpu.make_async_copy(pool.at[pl.ds(row0, rows)], buf.at[slot, pl.ds(p * rows, rows)],
                                         sem.at[slot]))
    return out


def _moba_sample_kernel(pt_ref, q_ref, slope_ref, qidx_ref, kn_ref, vn_ref, kpool, vpool, o_ref,
                        buf, kb, ksum, sem, *, base, n_pages, ppc, page, n_new, scale):
    b = pl.program_id(0)
    nch = n_pages // ppc
    ctok = ppc * page
    cblk = ctok // MOBA_BLOCK
    past = n_pages * page
    nb = past // MOBA_BLOCK
    d = MOBA_HEAD_DIM
    G = MOBA_KV_HEADS
    copies = functools.partial(_page_copies, pt_ref=pt_ref, b=b, base=base, buf=buf, sem=sem, ppc=ppc,
                               rows=page * G)

    def head_rows(slot, g):
        return buf[slot, pl.ds(g, ctok, stride=G), :]

    def stream(pool, compute, carry):
        for c in copies(pool, chunk=0, slot=0):
            c.start()

        def body(c, carry):
            slot = c % 2
            for cp in copies(pool, chunk=c, slot=slot):
                cp.wait()

            @pl.when(c + 1 < nch)
            def _():
                for cp in copies(pool, chunk=c + 1, slot=1 - slot):
                    cp.start()
            return compute(c, slot, carry)
        return lax.fori_loop(0, nch, body, carry)

    def keys(c, slot, carry):
        for g in range(G):
            x = head_rows(slot, g)
            ksum[g, c] = jnp.sum(x.reshape(cblk, MOBA_BLOCK, d), axis=1)
            kb[g, pl.ds(pl.multiple_of(c * ctok, ctok), ctok), :] = x.astype(BF16)
        return carry
    stream(kpool, keys, 0)

    rows = q_ref.shape[2]
    lane = lax.broadcasted_iota(jnp.int32, (rows, LANES), 1)
    qidx = qidx_ref[:, 0:1]
    qs, slopes, picks = [], [], []
    for g in range(G):
        kmean = ksum[g].reshape(nb, d) * (1.0 / MOBA_BLOCK)
        kmean = jnp.concatenate([kmean, jnp.zeros((LANES - nb, d), F32)], axis=0)
        qf = q_ref[0, g]
        gate = _nt(qf, kmean, HI)
        gate = jnp.where(lane < nb, gate, -jnp.inf)
        picks.append(_top_blocks(gate, lane, MOBA_TOPK))
        qs.append(qf)
        slopes.append(slope_ref[g][:, 0:1])

    def values(c, slot, carry):
        kpos = c * ctok + lax.broadcasted_iota(jnp.int32, (rows, ctok), 1)
        kblk = kpos // MOBA_BLOCK
        out = []
        for g in range(G):
            m, l, acc = carry[g]
            kc = kb[g, pl.ds(pl.multiple_of(c * ctok, ctok), ctok), :]
            lg = _nt(qs[g].astype(BF16), kc) * scale
            lg = lg - slopes[g] * (past + qidx - kpos).astype(F32)
            allowed = (kblk == picks[g][0]) | (kblk == picks[g][1]) | (kblk == picks[g][2])
            m_new = jnp.maximum(m, jnp.max(jnp.where(allowed, lg, NEG), axis=-1, keepdims=True))
            p = jnp.where(allowed, jnp.exp(lg - m_new), 0.0)
            alpha = jnp.exp(m - m_new)
            l = l * alpha + jnp.sum(p, axis=-1, keepdims=True)
            acc = acc * alpha + jnp.dot(p.astype(BF16), head_rows(slot, g).astype(BF16),
                                        preferred_element_type=F32)
            out.append((m_new, l, acc))
        return tuple(out)
    init = tuple((jnp.full((rows, 1), NEG, F32), jnp.zeros((rows, 1), F32), jnp.zeros((rows, d), F32))
                 for _ in range(MOBA_KV_HEADS))
    state = stream(vpool, values, init)

    for g in range(MOBA_KV_HEADS):
        m, l, acc = state[g]
        for j in range(n_new):
            kj = kn_ref[0, j:j + 1, g * d:(g + 1) * d]
            vj = vn_ref[0, j:j + 1, g * d:(g + 1) * d]
            lg = jnp.sum(qs[g] * kj, axis=-1, keepdims=True) * scale - slopes[g] * (qidx - j).astype(F32)
            ok = qidx >= j
            m_new = jnp.maximum(m, jnp.where(ok, lg, NEG))
            p = jnp.where(ok, jnp.exp(lg - m_new), 0.0)
            alpha = jnp.exp(m - m_new)
            l = l * alpha + p
            acc = acc * alpha + p * vj
            m = m_new
        o_ref[0, g] = acc / l


def _moba_sample(q, k_new, v_new, k_pool, v_pool, layer, page_table, slopes):
    DB, Q, _ = q.shape
    n_phys, page = k_pool.shape[1:3]
    n_pages = page_table.shape[1]
    past = n_pages * page
    assert past % MOBA_BLOCK == 0 and MOBA_TOPK <= past // MOBA_BLOCK <= LANES
    d, grp = MOBA_HEAD_DIM, MOBA_HEADS // MOBA_KV_HEADS
    kvw = MOBA_KV_HEADS * d
    ppc = _tile(n_pages, 16, 2 * MOBA_BLOCK // page)
    rows = Q * grp
    qg = q.reshape(DB, Q, MOBA_KV_HEADS, grp, d).transpose(0, 2, 1, 3, 4).reshape(DB, MOBA_KV_HEADS, rows, d)
    slope_rows = np.broadcast_to(np.tile(slopes.reshape(MOBA_KV_HEADS, 1, grp), (1, Q, 1)).reshape(
        MOBA_KV_HEADS, rows, 1), (MOBA_KV_HEADS, rows, LANES))
    qidx_rows = np.broadcast_to(np.repeat(np.arange(Q, dtype=np.int32), grp)[:, None], (rows, LANES))
    fix2 = lambda b, pt: (0, 0)
    out = pl.pallas_call(
        functools.partial(_moba_sample_kernel, base=layer * n_phys, n_pages=n_pages, ppc=ppc, page=page, n_new=Q,
                          scale=d ** -0.5),
        out_shape=jax.ShapeDtypeStruct((DB, MOBA_KV_HEADS, rows, d), F32),
        grid_spec=pltpu.PrefetchScalarGridSpec(
            num_scalar_prefetch=1,
            grid=(DB,),
            in_specs=[pl.BlockSpec((1, MOBA_KV_HEADS, rows, d), lambda b, pt: (b, 0, 0, 0)),
                      pl.BlockSpec((MOBA_KV_HEADS, rows, LANES), lambda b, pt: (0, 0, 0)),
                      pl.BlockSpec((rows, LANES), fix2),
                      pl.BlockSpec((1, Q, kvw), lambda b, pt: (b, 0, 0)),
                      pl.BlockSpec((1, Q, kvw), lambda b, pt: (b, 0, 0)),
                      pl.BlockSpec(memory_space=pl.ANY),
                      pl.BlockSpec(memory_space=pl.ANY)],
            out_specs=pl.BlockSpec((1, MOBA_KV_HEADS, rows, d), lambda b, pt: (b, 0, 0, 0)),
            scratch_shapes=[pltpu.VMEM((2, ppc * page * MOBA_KV_HEADS, d), F32),
                            pltpu.VMEM((MOBA_KV_HEADS, past, d), BF16),
                            pltpu.VMEM((MOBA_KV_HEADS, n_pages // ppc, ppc * page // MOBA_BLOCK, d), F32),
                            pltpu.SemaphoreType.DMA((2,))]),
        compiler_params=_params("arbitrary"),
        name="moba_sample",
    )(page_table, qg, jnp.asarray(slope_rows), jnp.asarray(qidx_rows), k_new, v_new,
      k_pool.reshape(-1, d), v_pool.reshape(-1, d))
    return out.reshape(DB, MOBA_KV_HEADS, Q, grp, d).transpose(0, 2, 1, 3, 4).reshape(DB, Q, MOBA_HEADS * d)


MLA_HEAD_PAD = 16


def _mla_sample_kernel(pt_ref, qw_ref, latn_ref, cs_ref, csn_ref, qidx_ref, wukt_ref, pool, o_ref, buf, sem,
                       *, base, n_pages, ppc, page, n_new, scale):
    b = pl.program_id(0)
    nch = n_pages // ppc
    ctok = ppc * page
    r, rd, H = MLA_KV_RANK, MLA_ROPE_DIM, MLA_HEADS
    latw = r + rd
    copies = functools.partial(_page_copies, pool, pt_ref, b, base, buf=buf, sem=sem, ppc=ppc, rows=latw)
    qw = qw_ref[0]
    q_abs, q_cs = qw[:, :r], qw[:, r:]
    wukt = wukt_ref[...]
    rows = qw.shape[0]
    nq = rows // MLA_HEAD_PAD

    def attend(ct, kpet, cs, allowed, carry):
        m, l, acc = carry
        tk = ct.shape[1]
        cb = ct.astype(BF16)
        knt = jnp.dot(wukt, cb, preferred_element_type=F32)
        ssq = jnp.sum((knt * knt).reshape(H, MLA_NOPE_DIM, tk), axis=1) + jnp.sum(kpet * kpet, axis=0, keepdims=True)
        inv = lax.rsqrt(ssq * (1.0 / MLA_QK_DIM) + RMS_EPS)
        inv = jnp.concatenate([inv, jnp.zeros((MLA_HEAD_PAD - H, tk), F32)], axis=0)
        inv = jnp.concatenate([inv] * nq, axis=0)
        krot = (jnp.concatenate([kpet, kpet], axis=0) * cs).astype(BF16)
        raw = jnp.dot(q_abs, cb, preferred_element_type=F32) + jnp.dot(q_cs, krot, preferred_element_type=F32)
        lg = raw * inv * scale
        if allowed is not None:
            lg = jnp.where(allowed, lg, NEG)
        m_new = jnp.maximum(m, jnp.max(lg, axis=-1, keepdims=True))
        p = jnp.exp(lg - m_new)
        if allowed is not None:
            p = jnp.where(allowed, p, 0.0)
        alpha = jnp.exp(m - m_new)
        l = l * alpha + jnp.sum(p, axis=-1, keepdims=True)
        acc = acc * alpha + _nt(p.astype(BF16), cb)
        return m_new, l, acc

    for cp in copies(chunk=0, slot=0):
        cp.start()

    def body(c, carry):
        slot = c % 2
        for cp in copies(chunk=c, slot=slot):
            cp.wait()

        @pl.when(c + 1 < nch)
        def _():
            for cp in copies(chunk=c + 1, slot=1 - slot):
                cp.start()
        ct = jnp.concatenate([buf[slot, pl.ds(p * latw, r), :] for p in range(ppc)], axis=1)
        kpet = jnp.concatenate([buf[slot, pl.ds(p * latw + r, rd), :] for p in range(ppc)], axis=1)
        cs = cs_ref[:, pl.ds(pl.multiple_of(c * ctok, ctok), ctok)]
        return attend(ct, kpet, cs, None, carry)

    init = (jnp.full((rows, 1), NEG, F32), jnp.zeros((rows, 1), F32), jnp.zeros((rows, r), F32))
    state = lax.fori_loop(0, nch, body, init)
    tn = latn_ref.shape[2]
    j = lax.broadcasted_iota(jnp.int32, (rows, tn), 1)
    allowed = (j <= qidx_ref[:, 0:1]) & (j < n_new)
    m, l, acc = attend(latn_ref[0, :r, :], latn_ref[0, r:, :], csn_ref[...], allowed, state)
    o_ref[0] = acc / l


def _rope_cs(pos):
    half = MLA_ROPE_DIM // 2
    inv = ROPE_THETA ** (-jnp.arange(half, dtype=F32) / half)
    ang = pos.astype(F32)[:, None] * inv
    cos, sin = jnp.cos(ang), jnp.sin(ang)
    return jnp.concatenate([cos, cos], axis=-1), jnp.concatenate([sin, sin], axis=-1)


def _abs_q_kernel(q_ref, g_ref, w_ref, o_ref):
    qg = (q_ref[...] * g_ref[...]).astype(BF16)
    o_ref[0] = _nt(qg, w_ref[...]).astype(o_ref.dtype)


def _head_out_kernel(a_ref, w_ref, o_ref):
    o_ref[...] = jnp.dot(a_ref[0], w_ref[...], preferred_element_type=F32).astype(o_ref.dtype)


def _mla_sample(qcat, lat_new, latent_pool, layer, page_table, wuk_b, wuv_b, gk):
    DB, Q, _ = qcat.shape
    _, n_phys, page, latw = latent_pool.shape
    n_pages = page_table.shape[1]
    past = n_pages * page
    H, r, nd, rd = MLA_HEADS, MLA_KV_RANK, MLA_NOPE_DIM, MLA_ROPE_DIM
    HP = MLA_HEAD_PAD
    M = DB * Q
    rows = Q * HP
    assert page == LANES and H <= HP
    q4 = qcat.reshape(M, H, MLA_QK_PAD)
    q_abs = pl.pallas_call(
        _abs_q_kernel,
        out_shape=jax.ShapeDtypeStruct((H, M, r), BF16),
        grid=(H,),
        in_specs=[pl.BlockSpec((M, nd), lambda h: (0, 2 * h)), pl.BlockSpec((1, nd), lambda h: (0, 0)),
                  pl.BlockSpec((r, nd), lambda h: (0, h))],
        out_specs=pl.BlockSpec((1, M, r), lambda h: (h, 0, 0)),
        compiler_params=_params("parallel"),
        name="mla_abs_q",
    )(qcat.reshape(M, H * MLA_QK_PAD), gk[:nd].reshape(1, nd), wuk_b)
    qr = q4[:, :, nd:nd + rd]
    gr = gk[nd:]
    q_cos = qr * gr
    q_sin = jnp.concatenate([qr[..., rd // 2:], -qr[..., :rd // 2]], axis=-1) * gr
    qw = jnp.concatenate([q_abs.transpose(1, 0, 2), q_cos.astype(BF16), q_sin.astype(BF16)], axis=-1)
    qw = jnp.pad(qw, ((0, 0), (0, HP - H), (0, 0))).reshape(DB, rows, r + 2 * rd)
    tn = LANES
    latn = jnp.pad(lat_new.transpose(0, 2, 1), ((0, 0), (0, 0), (0, tn - Q)))
    cs = jnp.concatenate(_rope_cs(jnp.arange(past)), axis=-1).T
    csn = jnp.concatenate(_rope_cs(past + jnp.arange(tn)), axis=-1).T
    qidx = np.broadcast_to(np.repeat(np.arange(Q, dtype=np.int32), HP)[:, None], (rows, LANES))
    pool_t = jnp.swapaxes(latent_pool, 2, 3).reshape(-1, page)
    ppc = _tile(n_pages, 8, 1)
    fix2 = lambda b, pt: (0, 0)
    acc = pl.pallas_call(
        functools.partial(_mla_sample_kernel, base=layer * n_phys, n_pages=n_pages, ppc=ppc, page=page, n_new=Q,
                          scale=MLA_QK_DIM ** -0.5),
        out_shape=jax.ShapeDtypeStruct((DB, rows, r), F32),
        grid_spec=pltpu.PrefetchScalarGridSpec(
            num_scalar_prefetch=1,
            grid=(DB,),
            in_specs=[pl.BlockSpec((1, rows, r + 2 * rd), lambda b, pt: (b, 0, 0)),
                      pl.BlockSpec((1, latw, tn), lambda b, pt: (b, 0, 0)),
                      pl.BlockSpec((2 * rd, past), fix2), pl.BlockSpec((2 * rd, tn), fix2),
                      pl.BlockSpec((rows, LANES), fix2),
                      pl.BlockSpec((H * nd, r), fix2),
                      pl.BlockSpec(memory_space=pl.ANY)],
            out_specs=pl.BlockSpec((1, rows, r), lambda b, pt: (b, 0, 0)),
            scratch_shapes=[pltpu.VMEM((2, ppc * latw, page), F32), pltpu.SemaphoreType.DMA((2,))]),
        compiler_params=_params("arbitrary"),
        name="mla_sample",
    )(page_table, qw, latn, cs, csn, jnp.asarray(qidx), wuk_b.T, pool_t)
    a = acc.reshape(DB, Q, HP, r)[:, :, :H].transpose(2, 0, 1, 3).reshape(H, M, r).astype(BF16)
    return pl.pallas_call(
        _head_out_kernel,
        out_shape=jax.ShapeDtypeStruct((M, H * MLA_V_DIM), BF16),
        grid=(H,),
        in_specs=[pl.BlockSpec((1, M, r), lambda h: (h, 0, 0)), pl.BlockSpec((r, MLA_V_DIM), lambda h: (0, h))],
        out_specs=pl.BlockSpec((M, MLA_V_DIM), lambda h: (0, h)),
        compiler_params=_params("parallel"),
        name="mla_head_out",
    )(a, wuv_b)


def _merge_kernel(oa_ref, ob_ref, om_ref, wa_ref, wb_ref, wm_ref, ga_ref, gb_ref, gm_ref, o_ref):
    def branch(o, w, g):
        return jax.nn.sigmoid(g[...]) * jnp.dot(o[...], w[...], preferred_element_type=F32)
    o_ref[...] = (branch(oa_ref, wa_ref, ga_ref) + branch(ob_ref, wb_ref, gb_ref)
                  + branch(om_ref, wm_ref, gm_ref)).astype(o_ref.dtype)


def _merge(oa, ob, om, wa, wb, wm, gates):
    T, D = oa.shape[0], wa.shape[1]
    tm, tn = _tile(T, 512), _tile(D, 512, LANES)
    nj = D // tn
    row = lambda i, j: (i, 0)
    col = lambda i, j: (0, j)
    return pl.pallas_call(
        _merge_kernel,
        out_shape=jax.ShapeDtypeStruct((T, D), BF16),
        grid=(T // tm, nj),
        in_specs=[pl.BlockSpec((tm, oa.shape[1]), row), pl.BlockSpec((tm, ob.shape[1]), row),
                  pl.BlockSpec((tm, om.shape[1]), row),
                  pl.BlockSpec((wa.shape[0], tn), col), pl.BlockSpec((wb.shape[0], tn), col),
                  pl.BlockSpec((wm.shape[0], tn), col),
                  pl.BlockSpec((tm, tn), lambda i, j: (i, j)),
                  pl.BlockSpec((tm, tn), lambda i, j: (i, j + nj)),
                  pl.BlockSpec((tm, tn), lambda i, j: (i, j + 2 * nj))],
        out_specs=pl.BlockSpec((tm, tn), lambda i, j: (i, j)),
        compiler_params=_params("parallel", "parallel"),
        name="merge",
    )(oa, ob, om, wa, wb, wm, gates, gates, gates)


def _top_rows(s, row, k):
    n = s.shape[0]
    vals, idxs = [], []
    for _ in range(k):
        m = jnp.max(s, axis=0, keepdims=True)
        i = jnp.min(jnp.where(s == m, row, n), axis=0, keepdims=True)
        vals.append(m)
        idxs.append(i)
        s = jnp.where(row == i, -jnp.inf, s)
    return vals, idxs


def _peer_route_kernel(q_ref, sk_ref, e_ref, g_ref):
    nk, k = PEER_N_KEYS, PEER_TOPK
    tm = q_ref.shape[0]
    row = lax.broadcasted_iota(jnp.int32, (nk, tm), 0)
    sv, si = [], []
    for p in range(2):
        s = _nt(sk_ref[0, p], q_ref[:, p * LANES:(p + 1) * LANES], HI)
        vals, idxs = _top_rows(s, row, k)
        sv.append(vals)
        si.append(idxs)
    sv1 = jnp.concatenate(sv[1], axis=0)
    si1 = jnp.concatenate(si[1], axis=0)
    cand = jnp.concatenate([sv[0][a] + sv1 for a in range(k)], axis=0)
    cidx = jnp.concatenate([si[0][a] * nk + si1 for a in range(k)], axis=0)
    pos = lax.broadcasted_iota(jnp.int32, cand.shape, 0)
    tv, te = [], []
    for _ in range(k):
        m = jnp.max(cand, axis=0, keepdims=True)
        j = jnp.min(jnp.where(cand == m, pos, k * k), axis=0, keepdims=True)
        pick = pos == j
        te.append(jnp.max(jnp.where(pick, cidx, -1), axis=0, keepdims=True))
        tv.append(m)
        cand = jnp.where(pick, -jnp.inf, cand)
    ts = jnp.concatenate(tv, axis=0)
    ex = jnp.exp(ts - tv[0])
    e_ref[0] = jnp.concatenate(te, axis=0)
    g_ref[0] = ex / jnp.sum(ex, axis=0, keepdims=True)


def _peer_route(qp, subkeys):
    T = qp.shape[0]
    tm = _tile(T, 256, LANES) if T % LANES == 0 else T
    nt = T // tm
    slots = PEER_HEADS * PEER_TOPK
    e, g = pl.pallas_call(
        _peer_route_kernel,
        out_shape=(jax.ShapeDtypeStruct((nt, slots, tm), jnp.int32), jax.ShapeDtypeStruct((nt, slots, tm), F32)),
        grid=(nt, PEER_HEADS),
        in_specs=[pl.BlockSpec((tm, PEER_QUERY_DIM), lambda i, h: (i, h)),
                  pl.BlockSpec((1, 2, PEER_N_KEYS, PEER_QUERY_DIM // 2), lambda i, h: (h, 0, 0, 0))],
        out_specs=(pl.BlockSpec((1, PEER_TOPK, tm), lambda i, h: (i, h, 0)),
                   pl.BlockSpec((1, PEER_TOPK, tm), lambda i, h: (i, h, 0))),
        compiler_params=_params("parallel", "parallel"),
        name="peer_route",
    )(qp, subkeys)
    return e.transpose(0, 2, 1).reshape(T, slots), g.transpose(0, 2, 1).reshape(T, slots)


def _peer_kernel(h_ref, a_ref, b_ref, g_ref, u_ref, v_ref, x_ref, o_ref, wbuf, wtmp, *, grp):
    e = pl.program_id(1)
    tt = h_ref.shape[0]
    nk = PEER_N_KEYS
    te = u_ref.shape[0]

    @pl.when(e == 0)
    def _():
        o_ref[...] = x_ref[...]
        sub = lax.broadcasted_iota(jnp.int32, (nk, a_ref.shape[1]), 0)

        def build(t8, carry):
            t0 = pl.multiple_of(t8 * grp, grp)
            for kk in range(grp):
                a = a_ref[pl.ds(t0 + kk, 1), :]
                bb = b_ref[pl.ds(t0 + kk, 1), :]
                gg = g_ref[pl.ds(t0 + kk, 1), :]
                ahot = jnp.where(sub == a, 1.0, 0.0).astype(BF16)
                bw = jnp.where(sub == bb, gg, 0.0)
                bhi = bw.astype(BF16)
                blo = (bw - bhi.astype(F32)).astype(BF16)
                wtmp[kk] = _nt(ahot, bhi) + _nt(ahot, blo)
            for i in range(nk):
                wbuf[i, pl.ds(t0, grp), :] = wtmp[:, i, :]
            return carry
        lax.fori_loop(0, tt // grp, build, 0)

    z = _nt(h_ref[...], u_ref[...])
    act = 0.5 * z * (1.0 + lax.erf(z * math.sqrt(0.5)))
    w = jnp.concatenate([wbuf[e * (te // nk) + r] for r in range(te // nk)], axis=1)
    o_ref[...] += jnp.dot((w * act).astype(BF16), v_ref[...], preferred_element_type=F32)


def _peer(hn, eidx, gate, u_b, v_b, x):
    T, D = x.shape
    E = u_b.shape[0]
    nk = PEER_N_KEYS
    tt = _tile(T, 256)
    te = 4 * nk
    grp = 8
    assert tt % grp == 0
    row = lambda i, e: (i, 0)
    exp = lambda i, e: (e, 0)
    slots = eidx.shape[1]
    return pl.pallas_call(
        functools.partial(_peer_kernel, grp=grp),
        out_shape=jax.ShapeDtypeStruct((T, D), F32),
        grid=(T // tt, E // te),
        in_specs=[pl.BlockSpec((tt, D), row), pl.BlockSpec((tt, slots), row), pl.BlockSpec((tt, slots), row),
                  pl.BlockSpec((tt, slots), row), pl.BlockSpec((te, D), exp), pl.BlockSpec((te, D), exp),
                  pl.BlockSpec((tt, D), row)],
        out_specs=pl.BlockSpec((tt, D), row),
        scratch_shapes=[pltpu.VMEM((nk, tt, nk), F32), pltpu.VMEM((grp, nk, nk), F32)],
        compiler_params=_params("parallel", "arbitrary"),
        name="peer",
    )(hn, eidx // nk, eidx % nk, gate, u_b, v_b, x)


def _layer(layer, xp, xs, mem_prompt, moba_k_pool, moba_v_pool, latent_pool, c_mem_k, c_mem_v, page_table,
           norm_attn_g, norm_ffn_g, norm_mem_g, w_in, moba_q_norm_g, moba_k_norm_g,
           mla_cq_norm_g, mla_ckv_norm_g, w_mla_uq, w_mla_uk, w_mla_uv, mla_q_norm_g, mla_k_norm_g,
           w_mem_k, w_mem_v, mem_q_norm_g, mem_k_norm_g, w_moba_o, w_mla_o, w_mem_o, w_out,
           w_peer_q, peer_subkeys, peer_u, peer_v):
    B, S, D = xp.shape
    DB, Q, _ = xs.shape
    Tp, Ts = B * S, DB * Q
    n_pages = page_table.shape[1]
    past = n_pages * moba_k_pool.shape[2]
    slopes = _alibi_slopes(MOBA_HEADS)
    qa_w, kv_w = MOBA_HEADS * MOBA_HEAD_DIM, MOBA_KV_HEADS * MOBA_HEAD_DIM
    qm_w = MEM_HEADS * MEM_HEAD_DIM
    cuts = np.cumsum([qa_w, kv_w, kv_w, MLA_Q_RANK, MLA_KV_RANK, MLA_ROPE_DIM, qm_w])
    c_qa, c_ka, c_va, c_cq, c_ckv, c_kpe, c_qm = [int(c) for c in cuts]

    x = jnp.concatenate([xp.reshape(Tp, D), xs.reshape(Ts, D)], axis=0)
    pos = jnp.concatenate([jnp.tile(jnp.arange(S), B), jnp.tile(past + jnp.arange(Q), DB)])
    t1, t2 = _rope_tables(pos)

    hn = _headnorm(x, norm_attn_g, D, BF16)
    w_attn = jnp.concatenate([w_in[:, :c_ckv], w_in[:, c_kpe:c_qm]], axis=1).astype(BF16)
    w_kpe2 = _rope_dup(w_in[:, c_ckv:c_kpe]).astype(BF16)
    pa = _mm(hn, w_attn)
    kpe2 = _mm(hn, w_kpe2)
    gates = _mm(hn, w_in[:, c_qm:].astype(BF16))
    qa = _headnorm(pa[:, :c_qa], moba_q_norm_g, MOBA_HEAD_DIM, F32)
    ka = _headnorm(pa[:, c_qa:c_ka], moba_k_norm_g, MOBA_HEAD_DIM, F32)
    va = pa[:, c_ka:c_va]
    cq = _headnorm(pa[:, c_va:c_cq], mla_cq_norm_g, MLA_Q_RANK, BF16)
    ckv = _headnorm(pa[:, c_cq:c_ckv], mla_ckv_norm_g, MLA_KV_RANK, F32)
    qm = _headnorm(pa[:, c_ckv:], mem_q_norm_g, MEM_HEAD_DIM, BF16)
    kpe = kpe2[:, :MLA_ROPE_DIM]
    lat = jnp.concatenate([ckv, kpe], axis=-1)

    h_idx = np.arange(MLA_HEADS)[:, None] * MLA_QK_DIM
    half = MLA_ROPE_DIM // 2
    r_idx = MLA_NOPE_DIM + np.concatenate([np.arange(MLA_ROPE_DIM), np.arange(half, MLA_ROPE_DIM), np.arange(half)])
    cols = (h_idx + np.concatenate([np.arange(MLA_NOPE_DIM), r_idx])[None, :]).reshape(-1)
    qcat = _mla_q_prep(_mm(cq, w_mla_uq[:, cols].astype(BF16)), t1, t2, mla_q_norm_g)
    wuk_b, wuv_b = w_mla_uk.astype(BF16), w_mla_uv.astype(BF16)

    o_a = _moba_prompt(qa[:Tp].reshape(B, S, qa_w), ka[:Tp].reshape(B, S, kv_w), va[:Tp].reshape(B, S, kv_w), slopes)
    kvp = _mm(ckv[:Tp].astype(BF16), jnp.concatenate([wuk_b, wuv_b], axis=1))
    kb, vb = _mla_kv_prep(kvp, kpe2[:Tp], t1[:Tp], t2[:Tp], mla_k_norm_g)
    o_b = _attn(qcat[:Tp].reshape(B, S, -1), kb.reshape(B, S, -1), vb.reshape(B, S, -1), heads=MLA_HEADS,
                dk=MLA_QK_PAD, dv=MLA_V_DIM, scale=MLA_QK_DIM ** -0.5, causal=True)
    M = mem_prompt.shape[1]
    mn = _headnorm(mem_prompt.reshape(B * M, D), norm_mem_g, D, BF16)
    mkv = _mm(mn, jnp.concatenate([w_mem_k, w_mem_v], axis=1).astype(BF16))
    mk = _headnorm(mkv[:, :qm_w], mem_k_norm_g, MEM_HEAD_DIM, F32)
    mv = mkv[:, qm_w:]
    mem_scale = MEM_HEAD_DIM ** -0.5
    o_m = _attn(qm[:Tp].reshape(B, S, qm_w), mk.reshape(B, M, qm_w), mv.reshape(B, M, qm_w), heads=MEM_HEADS,
                dk=MEM_HEAD_DIM, dv=MEM_HEAD_DIM, scale=mem_scale, causal=False)

    o_a_s = _moba_sample(qa[Tp:].reshape(DB, Q, qa_w), ka[Tp:].reshape(DB, Q, kv_w), va[Tp:].reshape(DB, Q, kv_w),
                         moba_k_pool, moba_v_pool, layer, page_table, slopes)
    o_b_s = _mla_sample(qcat[Tp:].reshape(DB, Q, -1), lat[Tp:].reshape(DB, Q, -1), latent_pool, layer, page_table,
                        wuk_b, wuv_b, mla_k_norm_g)
    qpad = 8
    qm_s = jnp.pad(qm[Tp:].reshape(DB, Q, qm_w), ((0, 0), (0, qpad - Q), (0, 0)))
    Ms = c_mem_k.shape[1]
    o_m_s = _attn(qm_s, c_mem_k.reshape(DB, Ms, qm_w), c_mem_v.reshape(DB, Ms, qm_w), heads=MEM_HEADS,
                  dk=MEM_HEAD_DIM, dv=MEM_HEAD_DIM, scale=mem_scale, causal=False)[:, :Q]

    oa = jnp.concatenate([o_a.reshape(Tp, -1), o_a_s.reshape(Ts, -1).astype(BF16)], axis=0)
    ob = jnp.concatenate([o_b.reshape(Tp, -1), o_b_s], axis=0)
    om = jnp.concatenate([o_m.reshape(Tp, -1), o_m_s.reshape(Ts, -1)], axis=0)
    mg = _merge(oa, ob, om, w_moba_o.astype(BF16), w_mla_o.astype(BF16), w_mem_o.astype(BF16), gates)
    x1 = _mm(mg, w_out.astype(BF16), residual=x)

    hf = _headnorm(x1, norm_ffn_g, D, BF16)
    eidx, gate = _peer_route(_mm(hf, w_peer_q.astype(BF16)), peer_subkeys)
    y = _peer(hf, eidx, gate, peer_u.astype(BF16), peer_v.astype(BF16), x1)

    kvh = (MOBA_KV_HEADS, MOBA_HEAD_DIM)
    states = (ka[:Tp].reshape(B, S, *kvh), va[:Tp].reshape(B, S, *kvh), lat[:Tp].reshape(B, S, -1),
              mk.reshape(B, M, MEM_HEADS, MEM_HEAD_DIM), mv.reshape(B, M, MEM_HEADS, MEM_HEAD_DIM),
              ka[Tp:].reshape(DB, Q, *kvh), va[Tp:].reshape(DB, Q, *kvh), lat[Tp:].reshape(DB, Q, -1))
    return y[:Tp].reshape(B, S, D), y[Tp:].reshape(DB, Q, D), states


def kernel(x_prompt, x_sample, mem_prompt, cache_moba_k, cache_moba_v, cache_mla_latent, cache_mem_k, cache_mem_v, page_table, norm_attn_g, norm_ffn_g, norm_mem_g, w_in, moba_q_norm_g, moba_k_norm_g, mla_cq_norm_g, mla_ckv_norm_g, w_mla_uq, w_mla_uk, w_mla_uv, mla_q_norm_g, mla_k_norm_g, w_mem_k, w_mem_v, mem_q_norm_g, mem_k_norm_g, w_moba_o, w_mla_o, w_mem_o, w_out, w_peer_q, peer_subkeys, peer_u, peer_v):
    xp, xs = x_prompt, x_sample
    new = [[] for _ in range(8)]
    for l in range(w_in.shape[0]):
        xp, xs, states = _layer(
            l, xp, xs, mem_prompt, cache_moba_k, cache_moba_v, cache_mla_latent, cache_mem_k[l],
            cache_mem_v[l], page_table, norm_attn_g[l], norm_ffn_g[l], norm_mem_g[l], w_in[l],
            moba_q_norm_g[l], moba_k_norm_g[l], mla_cq_norm_g[l], mla_ckv_norm_g[l], w_mla_uq[l], w_mla_uk[l],
            w_mla_uv[l], mla_q_norm_g[l], mla_k_norm_g[l], w_mem_k[l], w_mem_v[l], mem_q_norm_g[l],
            mem_k_norm_g[l], w_moba_o[l], w_mla_o[l], w_mem_o[l], w_out[l], w_peer_q[l], peer_subkeys[l],
            peer_u[l], peer_v[l])
        for lst, val in zip(new, states):
            lst.append(val)
    return (xp, xs) + tuple(jnp.stack(v) for v in new)
```

```python
import functools
import math

import numpy as np
import jax
import jax.numpy as jnp
from jax import lax
from jax.experimental import pallas as pl
from jax.experimental.pallas import tpu as pltpu

MOBA_HEADS = 12
MOBA_KV_HEADS = 2
MOBA_HEAD_DIM = 128
MOBA_BLOCK = 256
MOBA_TOPK = 3
MLA_HEADS = 12
MLA_Q_RANK = 768
MLA_KV_RANK = 512
MLA_NOPE_DIM = 128
MLA_ROPE_DIM = 64
MLA_QK_DIM = MLA_NOPE_DIM + MLA_ROPE_DIM
MLA_V_DIM = 128
ROPE_THETA = 10000.0
MEM_HEADS = 4
MEM_HEAD_DIM = 256
PEER_HEADS = 8
PEER_N_KEYS = 128
PEER_QUERY_DIM = 256
PEER_TOPK = 16
RMS_EPS = 1e-6

LANES = 128
MLA_QK_PAD = 2 * LANES
VMEM_LIMIT = 56 * 1024 * 1024
NEG = -1e30
RING_SLOTS = 4
HI = lax.Precision.HIGHEST
F32 = jnp.float32
BF16 = jnp.bfloat16
NT_DIMS = (((1,), (1,)), ((), ()))


def _tile(n, pref, align=8):
    t = (min(pref, n) // align) * align
    while t >= align:
        if n % t == 0:
            return t
        t -= align
    return n


def _params(*sem):
    return pltpu.CompilerParams(dimension_semantics=sem, vmem_limit_bytes=VMEM_LIMIT)


def _nt(a, b, precision=None):
    return lax.dot_general(a, b, NT_DIMS, precision=precision, preferred_element_type=F32)


def _headnorm_kernel(x_ref, g_ref, o_ref, *, hd):
    g = g_ref[...]
    for h in range(x_ref.shape[1] // hd):
        x = x_ref[:, h * hd:(h + 1) * hd]
        inv = lax.rsqrt(jnp.mean(x * x, axis=-1, keepdims=True) + RMS_EPS)
        o_ref[:, h * hd:(h + 1) * hd] = (x * inv * g).astype(o_ref.dtype)


def _headnorm(x, g, hd, out_dtype):
    T, W = x.shape
    tm = _tile(T, 256)
    return pl.pallas_call(
        functools.partial(_headnorm_kernel, hd=hd),
        out_shape=jax.ShapeDtypeStruct((T, W), out_dtype),
        grid=(T // tm,),
        in_specs=[pl.BlockSpec((tm, W), lambda i: (i, 0)), pl.BlockSpec((1, hd), lambda i: (0, 0))],
        out_specs=pl.BlockSpec((tm, W), lambda i: (i, 0)),
        compiler_params=_params("parallel"),
        name="headnorm",
    )(x, g.reshape(1, hd).astype(F32))


def _mm_kernel(a_ref, w_ref, o_ref):
    o_ref[...] = jnp.dot(a_ref[...], w_ref[...], preferred_element_type=F32).astype(o_ref.dtype)


def _mm_res_kernel(a_ref, w_ref, r_ref, o_ref):
    o_ref[...] = r_ref[...] + jnp.dot(a_ref[...], w_ref[...], preferred_element_type=F32)


def _mm(a, w, *, residual=None, tm=512, tn=512, out_dtype=F32):
    M, K = a.shape
    N = w.shape[1]
    tm, tn = _tile(M, tm), _tile(N, tn, LANES)
    in_specs = [pl.BlockSpec((tm, K), lambda i, j: (i, 0)), pl.BlockSpec((K, tn), lambda i, j: (0, j))]
    args = [a, w]
    body = _mm_kernel
    if residual is not None:
        in_specs.append(pl.BlockSpec((tm, tn), lambda i, j: (i, j)))
        args.append(residual)
        body = _mm_res_kernel
    return pl.pallas_call(
        body,
        out_shape=jax.ShapeDtypeStruct((M, N), out_dtype),
        grid=(M // tm, N // tn),
        in_specs=in_specs,
        out_specs=pl.BlockSpec((tm, tn), lambda i, j: (i, j)),
        compiler_params=_params("parallel", "parallel"),
        name="mm",
    )(*args)


def _rope_tables(pos):
    half = MLA_ROPE_DIM // 2
    inv = ROPE_THETA ** (-jnp.arange(half, dtype=F32) / half)
    ang = pos.astype(F32)[:, None] * inv
    cos, sin = jnp.cos(ang), jnp.sin(ang)
    z = jnp.zeros_like(cos)
    return jnp.concatenate([cos, cos, z, z], axis=-1), jnp.concatenate([-sin, sin, z, z], axis=-1)


def _rope_dup(v):
    half = MLA_ROPE_DIM // 2
    return jnp.concatenate([v, v[..., half:], v[..., :half]], axis=-1)


def _mla_q_kernel(q_ref, t1_ref, t2_ref, gn_ref, g2_ref, o_ref):
    t1, t2, gn, g2 = t1_ref[...], t2_ref[...], gn_ref[...], g2_ref[...]
    first = lax.broadcasted_iota(jnp.int32, t1.shape, 1) < MLA_ROPE_DIM
    for h in range(MLA_HEADS):
        lo = h * MLA_QK_PAD
        n = q_ref[:, lo:lo + LANES]
        r = q_ref[:, lo + LANES:lo + MLA_QK_PAD]
        ss = (jnp.sum(n * n, axis=-1, keepdims=True)
              + jnp.sum(jnp.where(first, r * r, 0.0), axis=-1, keepdims=True))
        inv = lax.rsqrt(ss * (1.0 / MLA_QK_DIM) + RMS_EPS)
        rn = r * inv * g2
        o_ref[:, lo:lo + LANES] = n * inv * gn
        o_ref[:, lo + LANES:lo + MLA_QK_PAD] = rn * t1 + pltpu.roll(rn, MLA_ROPE_DIM, axis=1) * t2


def _mla_q_prep(qraw, t1, t2, g):
    T, W = qraw.shape
    tm = _tile(T, 256)
    row = lambda i: (i, 0)
    fix = lambda i: (0, 0)
    return pl.pallas_call(
        _mla_q_kernel,
        out_shape=jax.ShapeDtypeStruct((T, W), F32),
        grid=(T // tm,),
        in_specs=[pl.BlockSpec((tm, W), row), pl.BlockSpec((tm, LANES), row), pl.BlockSpec((tm, LANES), row),
                  pl.BlockSpec((1, LANES), fix), pl.BlockSpec((1, LANES), fix)],
        out_specs=pl.BlockSpec((tm, W), row),
        compiler_params=_params("parallel"),
        name="mla_q_prep",
    )(qraw, t1, t2, g[:MLA_NOPE_DIM].reshape(1, LANES), _rope_dup(g[MLA_NOPE_DIM:]).reshape(1, LANES))


def _mla_kv_kernel(kv_ref, kpe_ref, t1_ref, t2_ref, gn_ref, g2_ref, k_ref, v_ref):
    t1, t2, gn, g2 = t1_ref[...], t2_ref[...], gn_ref[...], g2_ref[...]
    kpe = kpe_ref[...]
    first = lax.broadcasted_iota(jnp.int32, t1.shape, 1) < MLA_ROPE_DIM
    sspe = jnp.sum(jnp.where(first, kpe * kpe, 0.0), axis=-1, keepdims=True)
    kr = kpe * g2
    kr = kr * t1 + pltpu.roll(kr, MLA_ROPE_DIM, axis=1) * t2
    nk = MLA_HEADS * MLA_NOPE_DIM
    for h in range(MLA_HEADS):
        n = kv_ref[:, h * LANES:(h + 1) * LANES]
        inv = lax.rsqrt((jnp.sum(n * n, axis=-1, keepdims=True) + sspe) * (1.0 / MLA_QK_DIM) + RMS_EPS)
        lo = h * MLA_QK_PAD
        k_ref[:, lo:lo + LANES] = (n * inv * gn).astype(k_ref.dtype)
        k_ref[:, lo + LANES:lo + MLA_QK_PAD] = (kr * inv).astype(k_ref.dtype)
    v_ref[...] = kv_ref[:, nk:].astype(v_ref.dtype)


def _mla_kv_prep(kv, kpe2, t1, t2, g):
    T = kv.shape[0]
    tm = _tile(T, 256)
    row = lambda i: (i, 0)
    fix = lambda i: (0, 0)
    kw, vw = MLA_HEADS * MLA_QK_PAD, MLA_HEADS * MLA_V_DIM
    return pl.pallas_call(
        _mla_kv_kernel,
        out_shape=(jax.ShapeDtypeStruct((T, kw), BF16), jax.ShapeDtypeStruct((T, vw), BF16)),
        grid=(T // tm,),
        in_specs=[pl.BlockSpec((tm, kv.shape[1]), row), pl.BlockSpec((tm, LANES), row),
                  pl.BlockSpec((tm, LANES), row), pl.BlockSpec((tm, LANES), row),
                  pl.BlockSpec((1, LANES), fix), pl.BlockSpec((1, LANES), fix)],
        out_specs=(pl.BlockSpec((tm, kw), row), pl.BlockSpec((tm, vw), row)),
        compiler_params=_params("parallel"),
        name="mla_kv_prep",
    )(kv, kpe2, t1, t2, g[:MLA_NOPE_DIM].reshape(1, LANES), _rope_dup(g[MLA_NOPE_DIM:]).reshape(1, LANES))


def _attn_kernel(q_ref, k_ref, v_ref, o_ref, *, scale, causal):
    q = q_ref[0].astype(BF16)
    k = k_ref[0].astype(BF16)
    lg = _nt(q, k) * scale
    if causal:
        tq = q.shape[0]
        qpos = pl.program_id(2) * tq + lax.broadcasted_iota(jnp.int32, lg.shape, 0)
        kpos = lax.broadcasted_iota(jnp.int32, lg.shape, 1)
        lg = jnp.where(kpos <= qpos, lg, -jnp.inf)
    m = jnp.max(lg, axis=-1, keepdims=True)
    p = jnp.exp(lg - m)
    l = jnp.sum(p, axis=-1, keepdims=True)
    o = jnp.dot(p.astype(BF16), v_ref[0].astype(BF16), preferred_element_type=F32)
    o_ref[0] = (o / l).astype(o_ref.dtype)


def _attn(q, k, v, *, heads, dk, dv, scale, causal, out_dtype=BF16, tq=256):
    B, S, _ = q.shape
    Sk = k.shape[1]
    tq = _tile(S, tq)
    return pl.pallas_call(
        functools.partial(_attn_kernel, scale=scale, causal=causal),
        out_shape=jax.ShapeDtypeStruct((B, S, heads * dv), out_dtype),
        grid=(B, heads, S // tq),
        in_specs=[pl.BlockSpec((1, tq, dk), lambda b, h, i: (b, i, h)),
                  pl.BlockSpec((1, Sk, dk), lambda b, h, i: (b, 0, h)),
                  pl.BlockSpec((1, Sk, dv), lambda b, h, i: (b, 0, h))],
        out_specs=pl.BlockSpec((1, tq, dv), lambda b, h, i: (b, i, h)),
        compiler_params=_params("parallel", "parallel", "parallel"),
        name="attn",
    )(q, k, v)


def _alibi_slopes(n):
    def pow2(m):
        start = 2.0 ** (-8.0 / m)
        return [start ** (i + 1) for i in range(m)]
    p = 2 ** int(math.floor(math.log2(n)))
    s = pow2(p)
    if p < n:
        s = s + pow2(2 * p)[0::2][: n - p]
    return np.array(s, dtype=np.float32)


def _top_blocks(gate, lane, k):
    width = gate.shape[1]
    picks = []
    for _ in range(k):
        m = jnp.max(gate, axis=-1, keepdims=True)
        idx = jnp.min(jnp.where(gate == m, lane, width), axis=-1, keepdims=True)
        idx = jnp.where(m > -jnp.inf, idx, -1)
        picks.append(idx)
        gate = jnp.where(lane == idx, -jnp.inf, gate)
    return picks


def _moba_prompt_kernel(q_ref, k_ref, v_ref, slope_ref, o_ref, *, scale):
    qf = q_ref[0]
    kf = k_ref[0]
    tq, S = qf.shape[0], kf.shape[0]
    nblk = S // MOBA_BLOCK
    kmean = jnp.mean(kf.reshape(nblk, MOBA_BLOCK, MOBA_HEAD_DIM), axis=1)
    kmean = jnp.concatenate([kmean, jnp.zeros((LANES - nblk, MOBA_HEAD_DIM), F32)], axis=0)
    gate = _nt(qf, kmean, HI)
    q0 = pl.program_id(2) * tq
    own = (q0 + lax.broadcasted_iota(jnp.int32, (tq, 1), 0)) // MOBA_BLOCK
    lane = lax.broadcasted_iota(jnp.int32, gate.shape, 1)
    gate = jnp.where(lane < own, gate, -jnp.inf)
    picks = _top_blocks(gate, lane, MOBA_TOPK)

    lg = _nt(qf.astype(BF16), kf.astype(BF16)) * scale
    qpos = q0 + lax.broadcasted_iota(jnp.int32, lg.shape, 0)
    kpos = lax.broadcasted_iota(jnp.int32, lg.shape, 1)
    slope = slope_ref[0, 0:1, 0:1]
    lg = lg - slope * (qpos - kpos).astype(F32)
    kblk = kpos // MOBA_BLOCK
    allowed = (kblk == own) & (kpos <= qpos)
    for idx in picks:
        allowed = allowed | (kblk == idx)
    lg = jnp.where(allowed, lg, -jnp.inf)
    m = jnp.max(lg, axis=-1, keepdims=True)
    p = jnp.exp(lg - m)
    l = jnp.sum(p, axis=-1, keepdims=True)
    o = jnp.dot(p.astype(BF16), v_ref[0].astype(BF16), preferred_element_type=F32)
    o_ref[0] = (o / l).astype(o_ref.dtype)


def _moba_prompt(q, k, v, slopes):
    B, S, _ = q.shape
    assert S % MOBA_BLOCK == 0 and S // MOBA_BLOCK <= LANES
    tq = _tile(S, 256)
    grp = MOBA_HEADS // MOBA_KV_HEADS
    d = MOBA_HEAD_DIM
    return pl.pallas_call(
        functools.partial(_moba_prompt_kernel, scale=d ** -0.5),
        out_shape=jax.ShapeDtypeStruct((B, S, MOBA_HEADS * d), BF16),
        grid=(B, MOBA_HEADS, S // tq),
        in_specs=[pl.BlockSpec((1, tq, d), lambda b, h, i: (b, i, h)),
                  pl.BlockSpec((1, S, d), lambda b, h, i: (b, 0, h // grp)),
                  pl.BlockSpec((1, S, d), lambda b, h, i: (b, 0, h // grp)),
                  pl.BlockSpec((1, 8, LANES), lambda b, h, i: (h, 0, 0))],
        out_specs=pl.BlockSpec((1, tq, d), lambda b, h, i: (b, i, h)),
        compiler_params=_params("parallel", "parallel", "parallel"),
        name="moba_prompt",
    )(q, k, v, jnp.broadcast_to(jnp.asarray(slopes)[:, None, None], (MOBA_HEADS, 8, LANES)))


def _page_copies(pool, pt_ref, b, base, chunk, buf, slot, sem, ppc, rows):
    out = []
    for p in range(ppc):
        row0 = pl.multiple_of((base + pt_ref[b, chunk * ppc + p]) * rows, rows)
        out.append(pltpu.make_async_copy(pool.at[pl.ds(row0, rows)], buf.at[slot, pl.ds(p * rows, rows)],
                                         sem.at[slot]))
    return out


def _ring_prime(copies, nch, ns):
    for c in range(min(ns - 1, nch)):
        for cp in copies(chunk=c, slot=c % ns):
            cp.start()


def _ring_run(copies, nch, ns, compute, carry):
    def body(c, carry):
        slot = c % ns
        for cp in copies(chunk=c, slot=slot):
            cp.wait()

        @pl.when(c + ns - 1 < nch)
        def _():
            for cp in copies(chunk=c + ns - 1, slot=(c + ns - 1) % ns):
                cp.start()
        return compute(c, slot, carry)
    return lax.fori_loop(0, nch, body, carry)


def _moba_sample_kernel(pt_ref, q_ref, slope_ref, qidx_ref, kn_ref, vn_ref, kpool, vpool, o_ref,
                        buf, kb, ksum, sem, *, base, n_pages, ppc, page, n_new, scale):
    b = pl.program_id(0)
    nseq = pl.num_programs(0)
    ns = buf.shape[0]
    nch = n_pages // ppc
    ctok = ppc * page
    cblk = ctok // MOBA_BLOCK
    past = n_pages * page
    nb = past // MOBA_BLOCK
    d = MOBA_HEAD_DIM
    G = MOBA_KV_HEADS

    def copies(pool, seq):
        return functools.partial(_page_copies, pool, pt_ref, seq, base, buf=buf, sem=sem, ppc=ppc, rows=page * G)

    def head_rows(slot, g):
        return buf[slot, pl.ds(g, ctok, stride=G), :]

    def keys(c, slot, carry):
        for g in range(G):
            x = head_rows(slot, g)
            ksum[g, c] = jnp.sum(x.reshape(cblk, MOBA_BLOCK, d), axis=1)
            kb[g, pl.ds(pl.multiple_of(c * ctok, ctok), ctok), :] = x.astype(BF16)
        return carry

    @pl.when(b == 0)
    def _():
        _ring_prime(copies(kpool, b), nch, ns)
    _ring_run(copies(kpool, b), nch, ns, keys, 0)
    _ring_prime(copies(vpool, b), nch, ns)

    rows = q_ref.shape[2]
    lane = lax.broadcasted_iota(jnp.int32, (rows, LANES), 1)
    qidx = qidx_ref[:, 0:1]
    qs, slopes, picks = [], [], []
    for g in range(G):
        kmean = ksum[g].reshape(nb, d) * (1.0 / MOBA_BLOCK)
        kmean = jnp.concatenate([kmean, jnp.zeros((LANES - nb, d), F32)], axis=0)
        qf = q_ref[0, g]
        gate = _nt(qf, kmean, HI)
        gate = jnp.where(lane < nb, gate, -jnp.inf)
        picks.append(_top_blocks(gate, lane, MOBA_TOPK))
        qs.append(qf)
        slopes.append(slope_ref[g][:, 0:1])

    def values(c, slot, carry):
        kpos = c * ctok + lax.broadcasted_iota(jnp.int32, (rows, ctok), 1)
        kblk = kpos // MOBA_BLOCK
        out = []
        for g in range(G):
            m, l, acc = carry[g]
            kc = kb[g, pl.ds(pl.multiple_of(c * ctok, ctok), ctok), :]
            lg = _nt(qs[g].astype(BF16), kc) * scale
            lg = lg - slopes[g] * (past + qidx - kpos).astype(F32)
            allowed = (kblk == picks[g][0]) | (kblk == picks[g][1]) | (kblk == picks[g][2])
            m_new = jnp.maximum(m, jnp.max(jnp.where(allowed, lg, NEG), axis=-1, keepdims=True))
            p = jnp.where(allowed, jnp.exp(lg - m_new), 0.0)
            alpha = jnp.exp(m - m_new)
            l = l * alpha + jnp.sum(p, axis=-1, keepdims=True)
            acc = acc * alpha + jnp.dot(p.astype(BF16), head_rows(slot, g).astype(BF16),
                                        preferred_element_type=F32)
            out.append((m_new, l, acc))
        return tuple(out)
    init = tuple((jnp.full((rows, 1), NEG, F32), jnp.zeros((rows, 1), F32), jnp.zeros((rows, d), F32))
                 for _ in range(MOBA_KV_HEADS))
    state = _ring_run(copies(vpool, b), nch, ns, values, init)

    @pl.when(b + 1 < nseq)
    def _():
        _ring_prime(copies(kpool, b + 1), nch, ns)

    for g in range(MOBA_KV_HEADS):
        m, l, acc = state[g]
        for j in range(n_new):
            kj = kn_ref[0, j:j + 1, g * d:(g + 1) * d]
            vj = vn_ref[0, j:j + 1, g * d:(g + 1) * d]
            lg = jnp.sum(qs[g] * kj, axis=-1, keepdims=True) * scale - slopes[g] * (qidx - j).astype(F32)
            ok = qidx >= j
            m_new = jnp.maximum(m, jnp.where(ok, lg, NEG))
            p = jnp.where(ok, jnp.exp(lg - m_new), 0.0)
            alpha = jnp.exp(m - m_new)
            l = l * alpha + p
            acc = acc * alpha + p * vj
            m = m_new
        o_ref[0, g] = acc / l


def _moba_sample(q, k_new, v_new, k_pool, v_pool, layer, page_table, slopes):
    DB, Q, _ = q.shape
    n_phys, page = k_pool.shape[1:3]
    n_pages = page_table.shape[1]
    past = n_pages * page
    assert past % MOBA_BLOCK == 0 and MOBA_TOPK <= past // MOBA_BLOCK <= LANES
    d, grp = MOBA_HEAD_DIM, MOBA_HEADS // MOBA_KV_HEADS
    kvw = MOBA_KV_HEADS * d
    ppc = _tile(n_pages, 16, 2 * MOBA_BLOCK // page)
    rows = Q * grp
    qg = q.reshape(DB, Q, MOBA_KV_HEADS, grp, d).transpose(0, 2, 1, 3, 4).reshape(DB, MOBA_KV_HEADS, rows, d)
    slope_rows = np.broadcast_to(np.tile(slopes.reshape(MOBA_KV_HEADS, 1, grp), (1, Q, 1)).reshape(
        MOBA_KV_HEADS, rows, 1), (MOBA_KV_HEADS, rows, LANES))
    qidx_rows = np.broadcast_to(np.repeat(np.arange(Q, dtype=np.int32), grp)[:, None], (rows, LANES))
    fix2 = lambda b, pt: (0, 0)
    out = pl.pallas_call(
        functools.partial(_moba_sample_kernel, base=layer * n_phys, n_pages=n_pages, ppc=ppc, page=page, n_new=Q,
                          scale=d ** -0.5),
        out_shape=jax.ShapeDtypeStruct((DB, MOBA_KV_HEADS, rows, d), F32),
        grid_spec=pltpu.PrefetchScalarGridSpec(
            num_scalar_prefetch=1,
            grid=(DB,),
            in_specs=[pl.BlockSpec((1, MOBA_KV_HEADS, rows, d), lambda b, pt: (b, 0, 0, 0)),
                      pl.BlockSpec((MOBA_KV_HEADS, rows, LANES), lambda b, pt: (0, 0, 0)),
                      pl.BlockSpec((rows, LANES), fix2),
                      pl.BlockSpec((1, Q, kvw), lambda b, pt: (b, 0, 0)),
                      pl.BlockSpec((1, Q, kvw), lambda b, pt: (b, 0, 0)),
                      pl.BlockSpec(memory_space=pl.ANY),
                      pl.BlockSpec(memory_space=pl.ANY)],
            out_specs=pl.BlockSpec((1, MOBA_KV_HEADS, rows, d), lambda b, pt: (b, 0, 0, 0)),
            scratch_shapes=[pltpu.VMEM((RING_SLOTS, ppc * page * MOBA_KV_HEADS, d), F32),
                            pltpu.VMEM((MOBA_KV_HEADS, past, d), BF16),
                            pltpu.VMEM((MOBA_KV_HEADS, n_pages // ppc, ppc * page // MOBA_BLOCK, d), F32),
                            pltpu.SemaphoreType.DMA((RING_SLOTS,))]),
        compiler_params=_params("arbitrary"),
        name="moba_sample",
    )(page_table, qg, jnp.asarray(slope_rows), jnp.asarray(qidx_rows), k_new, v_new,
      k_pool.reshape(-1, d), v_pool.reshape(-1, d))
    return out.reshape(DB, MOBA_KV_HEADS, Q, grp, d).transpose(0, 2, 1, 3, 4).reshape(DB, Q, MOBA_HEADS * d)


MLA_HEAD_PAD = 16


def _mla_sample_kernel(pt_ref, qw_ref, latn_ref, csn_ref, qidx_ref, wukt_ref, cs_hbm, pool, o_ref,
                       buf, csbuf, lhs, sem, cssem, *, base, n_pages, ppc, page, n_new, scale):
    b = pl.program_id(0)
    nseq = pl.num_programs(0)
    ns = buf.shape[0]
    nch = n_pages // ppc
    ctok = ppc * page
    r, rd, H = MLA_KV_RANK, MLA_ROPE_DIM, MLA_HEADS
    latw = r + rd
    qw = qw_ref[0]
    q_cs = qw[:, r:]
    rows = qw.shape[0]
    nq = rows // MLA_HEAD_PAD
    nkn = H * MLA_NOPE_DIM

    @pl.when(b == 0)
    def _():
        lhs[0:nkn, :] = wukt_ref[...]
    lhs[nkn:, :] = qw[:, :r]

    def copies(seq):
        def chunk_copies(chunk, slot):
            return (_page_copies(pool, pt_ref, seq, base, chunk, buf, slot, sem, ppc, latw)
                    + [pltpu.make_async_copy(cs_hbm.at[chunk], csbuf.at[slot], cssem.at[slot])])
        return chunk_copies

    def scores(ct, kpet, cs):
        tk = ct.shape[1]
        cb = ct.astype(BF16)
        prod = jnp.dot(lhs[...], cb, preferred_element_type=F32)
        knt = prod[:nkn]
        ssq = jnp.sum((knt * knt).reshape(H, MLA_NOPE_DIM, tk), axis=1) + jnp.sum(kpet * kpet, axis=0, keepdims=True)
        inv = lax.rsqrt(ssq * (1.0 / MLA_QK_DIM) + RMS_EPS)
        inv = jnp.concatenate([inv, jnp.zeros((MLA_HEAD_PAD - H, tk), F32)], axis=0)
        inv = jnp.concatenate([inv] * nq, axis=0)
        krot = (jnp.concatenate([kpet, kpet], axis=0) * cs).astype(BF16)
        raw = prod[nkn:] + jnp.dot(q_cs, krot, preferred_element_type=F32)
        return cb, raw * inv * scale

    def attend(ct, kpet, cs, allowed, carry):
        m, l, acc = carry
        cb, lg = scores(ct, kpet, cs)
        m_new = jnp.maximum(m, jnp.max(lg if allowed is None else jnp.where(allowed, lg, NEG), axis=-1, keepdims=True))
        p = jnp.exp(lg - m_new)
        if allowed is not None:
            p = jnp.where(allowed, p, 0.0)
        alpha = jnp.exp(m - m_new)
        return (m_new, l * alpha + jnp.sum(p, axis=-1, keepdims=True), acc * alpha + _nt(p.astype(BF16), cb))

    def page_attend(c, slot, carry):
        ct = jnp.concatenate([buf[slot, pl.ds(p * latw, r), :] for p in range(ppc)], axis=1)
        kpet = jnp.concatenate([buf[slot, pl.ds(p * latw + r, rd), :] for p in range(ppc)], axis=1)
        return attend(ct, kpet, csbuf[slot], None, carry)

    @pl.when(b == 0)
    def _():
        _ring_prime(copies(b), nch, ns)
    init = (jnp.full((rows, 1), NEG, F32), jnp.zeros((rows, 1), F32), jnp.zeros((rows, r), F32))
    state = _ring_run(copies(b), nch, ns, page_attend, init)

    @pl.when(b + 1 < nseq)
    def _():
        _ring_prime(copies(b + 1), nch, ns)

    tn = latn_ref.shape[2]
    j = lax.broadcasted_iota(jnp.int32, (rows, tn), 1)
    allowed = (j <= qidx_ref[:, 0:1]) & (j < n_new)
    m, l, acc = attend(latn_ref[0, :r, :], latn_ref[0, r:, :], csn_ref[...], allowed, state)
    o_ref[0] = acc / l


def _rope_cs(pos):
    half = MLA_ROPE_DIM // 2
    inv = ROPE_THETA ** (-jnp.arange(half, dtype=F32) / half)
    ang = pos.astype(F32)[:, None] * inv
    cos, sin = jnp.cos(ang), jnp.sin(ang)
    return jnp.concatenate([cos, cos], axis=-1), jnp.concatenate([sin, sin], axis=-1)


def _abs_q_kernel(q_ref, g_ref, w_ref, o_ref):
    qg = (q_ref[...] * g_ref[...]).astype(BF16)
    o_ref[0] = _nt(qg, w_ref[...]).astype(o_ref.dtype)


def _head_out_kernel(a_ref, w_ref, o_ref):
    o_ref[...] = jnp.dot(a_ref[0], w_ref[...], preferred_element_type=F32).astype(o_ref.dtype)


def _mla_sample(qcat, lat_new, latent_pool, layer, page_table, wuk_b, wuv_b, gk):
    DB, Q, _ = qcat.shape
    _, n_phys, page, latw = latent_pool.shape
    n_pages = page_table.shape[1]
    past = n_pages * page
    H, r, nd, rd = MLA_HEADS, MLA_KV_RANK, MLA_NOPE_DIM, MLA_ROPE_DIM
    HP = MLA_HEAD_PAD
    M = DB * Q
    rows = Q * HP
    assert page == LANES and H <= HP
    q4 = qcat.reshape(M, H, MLA_QK_PAD)
    q_abs = pl.pallas_call(
        _abs_q_kernel,
        out_shape=jax.ShapeDtypeStruct((H, M, r), BF16),
        grid=(H,),
        in_specs=[pl.BlockSpec((M, nd), lambda h: (0, 2 * h)), pl.BlockSpec((1, nd), lambda h: (0, 0)),
                  pl.BlockSpec((r, nd), lambda h: (0, h))],
        out_specs=pl.BlockSpec((1, M, r), lambda h: (h, 0, 0)),
        compiler_params=_params("parallel"),
        name="mla_abs_q",
    )(qcat.reshape(M, H * MLA_QK_PAD), gk[:nd].reshape(1, nd), wuk_b)
    qr = q4[:, :, nd:nd + rd]
    gr = gk[nd:]
    q_cos = qr * gr
    q_sin = jnp.concatenate([qr[..., rd // 2:], -qr[..., :rd // 2]], axis=-1) * gr
    qw = jnp.concatenate([q_abs.transpose(1, 0, 2), q_cos.astype(BF16), q_sin.astype(BF16)], axis=-1)
    qw = jnp.pad(qw, ((0, 0), (0, HP - H), (0, 0))).reshape(DB, rows, r + 2 * rd)
    tn = LANES
    latn = jnp.pad(lat_new.transpose(0, 2, 1), ((0, 0), (0, 0), (0, tn - Q)))
    cs = jnp.concatenate(_rope_cs(jnp.arange(past)), axis=-1).T
    csn = jnp.concatenate(_rope_cs(past + jnp.arange(tn)), axis=-1).T
    qidx = np.broadcast_to(np.repeat(np.arange(Q, dtype=np.int32), HP)[:, None], (rows, LANES))
    pool_t = jnp.swapaxes(latent_pool, 2, 3).reshape(-1, page)
    ppc = _tile(n_pages, 8, 1)
    nch, ctok = n_pages // ppc, ppc * page
    cs = cs.reshape(2 * rd, nch, ctok).transpose(1, 0, 2)
    fix2 = lambda b, pt: (0, 0)
    acc = pl.pallas_call(
        functools.partial(_mla_sample_kernel, base=layer * n_phys, n_pages=n_pages, ppc=ppc, page=page, n_new=Q,
                          scale=MLA_QK_DIM ** -0.5),
        out_shape=jax.ShapeDtypeStruct((DB, rows, r), F32),
        grid_spec=pltpu.PrefetchScalarGridSpec(
            num_scalar_prefetch=1,
            grid=(DB,),
            in_specs=[pl.BlockSpec((1, rows, r + 2 * rd), lambda b, pt: (b, 0, 0)),
                      pl.BlockSpec((1, latw, tn), lambda b, pt: (b, 0, 0)),
                      pl.BlockSpec((2 * rd, tn), fix2),
                      pl.BlockSpec((rows, LANES), fix2),
                      pl.BlockSpec((H * nd, r), fix2),
                      pl.BlockSpec(memory_space=pl.ANY),
                      pl.BlockSpec(memory_space=pl.ANY)],
            out_specs=pl.BlockSpec((1, rows, r), lambda b, pt: (b, 0, 0)),
            scratch_shapes=[pltpu.VMEM((RING_SLOTS, ppc * latw, page), F32),
                            pltpu.VMEM((RING_SLOTS, 2 * rd, ctok), F32),
                            pltpu.VMEM((H * nd + rows, r), BF16),
                            pltpu.SemaphoreType.DMA((RING_SLOTS,)),
                            pltpu.SemaphoreType.DMA((RING_SLOTS,))]),
        compiler_params=_params("arbitrary"),
        name="mla_sample",
    )(page_table, qw, latn, csn, jnp.asarray(qidx), wuk_b.T, cs, pool_t)
    a = acc.reshape(DB, Q, HP, r)[:, :, :H].transpose(2, 0, 1, 3).reshape(H, M, r).astype(BF16)
    return pl.pallas_call(
        _head_out_kernel,
        out_shape=jax.ShapeDtypeStruct((M, H * MLA_V_DIM), BF16),
        grid=(H,),
        in_specs=[pl.BlockSpec((1, M, r), lambda h: (h, 0, 0)), pl.BlockSpec((r, MLA_V_DIM), lambda h: (0, h))],
        out_specs=pl.BlockSpec((M, MLA_V_DIM), lambda h: (0, h)),
        compiler_params=_params("parallel"),
        name="mla_head_out",
    )(a, wuv_b)


def _merge_kernel(oa_ref, ob_ref, om_ref, wa_ref, wb_ref, wm_ref, ga_ref, gb_ref, gm_ref, o_ref):
    def branch(o, w, g):
        return jax.nn.sigmoid(g[...]) * jnp.dot(o[...], w[...], preferred_element_type=F32)
    o_ref[...] = (branch(oa_ref, wa_ref, ga_ref) + branch(ob_ref, wb_ref, gb_ref)
                  + branch(om_ref, wm_ref, gm_ref)).astype(o_ref.dtype)


def _merge(oa, ob, om, wa, wb, wm, gates):
    T, D = oa.shape[0], wa.shape[1]
    tm, tn = _tile(T, 512), _tile(D, 512, LANES)
    nj = D // tn
    row = lambda i, j: (i, 0)
    col = lambda i, j: (0, j)
    return pl.pallas_call(
        _merge_kernel,
        out_shape=jax.ShapeDtypeStruct((T, D), BF16),
        grid=(T // tm, nj),
        in_specs=[pl.BlockSpec((tm, oa.shape[1]), row), pl.BlockSpec((tm, ob.shape[1]), row),
                  pl.BlockSpec((tm, om.shape[1]), row),
                  pl.BlockSpec((wa.shape[0], tn), col), pl.BlockSpec((wb.shape[0], tn), col),
                  pl.BlockSpec((wm.shape[0], tn), col),
                  pl.BlockSpec((tm, tn), lambda i, j: (i, j)),
                  pl.BlockSpec((tm, tn), lambda i, j: (i, j + nj)),
                  pl.BlockSpec((tm, tn), lambda i, j: (i, j + 2 * nj))],
        out_specs=pl.BlockSpec((tm, tn), lambda i, j: (i, j)),
        compiler_params=_params("parallel", "parallel"),
        name="merge",
    )(oa, ob, om, wa, wb, wm, gates, gates, gates)


def _top_rows(s, row, k):
    n = s.shape[0]
    vals, idxs = [], []
    for _ in range(k):
        m = jnp.max(s, axis=0, keepdims=True)
        i = jnp.min(jnp.where(s == m, row, n), axis=0, keepdims=True)
        vals.append(m)
        idxs.append(i)
        s = jnp.where(row == i, -jnp.inf, s)
    return vals, idxs


def _peer_route_kernel(q_ref, sk_ref, e_ref, g_ref):
    nk, k = PEER_N_KEYS, PEER_TOPK
    tm = q_ref.shape[0]
    row = lax.broadcasted_iota(jnp.int32, (nk, tm), 0)
    sv, si = [], []
    for p in range(2):
        s = _nt(sk_ref[0, p], q_ref[:, p * LANES:(p + 1) * LANES], HI)
        vals, idxs = _top_rows(s, row, k)
        sv.append(vals)
        si.append(idxs)
    sv1 = jnp.concatenate(sv[1], axis=0)
    si1 = jnp.concatenate(si[1], axis=0)
    cand = jnp.concatenate([sv[0][a] + sv1 for a in range(k)], axis=0)
    cidx = jnp.concatenate([si[0][a] * nk + si1 for a in range(k)], axis=0)
    pos = lax.broadcasted_iota(jnp.int32, cand.shape, 0)
    tv, te = [], []
    for _ in range(k):
        m = jnp.max(cand, axis=0, keepdims=True)
        j = jnp.min(jnp.where(cand == m, pos, k * k), axis=0, keepdims=True)
        pick = pos == j
        te.append(jnp.max(jnp.where(pick, cidx, -1), axis=0, keepdims=True))
        tv.append(m)
        cand = jnp.where(pick, -jnp.inf, cand)
    ts = jnp.concatenate(tv, axis=0)
    ex = jnp.exp(ts - tv[0])
    e_ref[0] = jnp.concatenate(te, axis=0)
    g_ref[0] = ex / jnp.sum(ex, axis=0, keepdims=True)


def _peer_route(qp, subkeys):
    T = qp.shape[0]
    tm = _tile(T, 256, LANES) if T % LANES == 0 else T
    nt = T // tm
    slots = PEER_HEADS * PEER_TOPK
    e, g = pl.pallas_call(
        _peer_route_kernel,
        out_shape=(jax.ShapeDtypeStruct((nt, slots, tm), jnp.int32), jax.ShapeDtypeStruct((nt, slots, tm), F32)),
        grid=(nt, PEER_HEADS),
        in_specs=[pl.BlockSpec((tm, PEER_QUERY_DIM), lambda i, h: (i, h)),
                  pl.BlockSpec((1, 2, PEER_N_KEYS, PEER_QUERY_DIM // 2), lambda i, h: (h, 0, 0, 0))],
        out_specs=(pl.BlockSpec((1, PEER_TOPK, tm), lambda i, h: (i, h, 0)),
                   pl.BlockSpec((1, PEER_TOPK, tm), lambda i, h: (i, h, 0))),
        compiler_params=_params("parallel", "parallel"),
        name="peer_route",
    )(qp, subkeys)
    return e.transpose(0, 2, 1).reshape(T, slots), g.transpose(0, 2, 1).reshape(T, slots)


def _peer_kernel(h_ref, a_ref, b_ref, g_ref, u_ref, v_ref, x_ref, o_ref, wbuf, wtmp, *, grp):
    e = pl.program_id(1)
    tt = h_ref.shape[0]
    nk = PEER_N_KEYS
    te = u_ref.shape[0]

    @pl.when(e == 0)
    def _():
        o_ref[...] = x_ref[...]
        sub = lax.broadcasted_iota(jnp.int32, (nk, a_ref.shape[1]), 0)

        def build(t8, carry):
            t0 = pl.multiple_of(t8 * grp, grp)
            for kk in range(grp):
                a = a_ref[pl.ds(t0 + kk, 1), :]
                bb = b_ref[pl.ds(t0 + kk, 1), :]
                gg = g_ref[pl.ds(t0 + kk, 1), :]
                ahot = jnp.where(sub == a, 1.0, 0.0).astype(BF16)
                bw = jnp.where(sub == bb, gg, 0.0)
                bhi = bw.astype(BF16)
                blo = (bw - bhi.astype(F32)).astype(BF16)
                wtmp[kk] = _nt(ahot, bhi) + _nt(ahot, blo)
            for i in range(nk):
                wbuf[i, pl.ds(t0, grp), :] = wtmp[:, i, :]
            return carry
        lax.fori_loop(0, tt // grp, build, 0)

    z = _nt(h_ref[...], u_ref[...])
    act = 0.5 * z * (1.0 + lax.erf(z * math.sqrt(0.5)))
    w = jnp.concatenate([wbuf[e * (te // nk) + r] for r in range(te // nk)], axis=1)
    o_ref[...] += jnp.dot((w * act).astype(BF16), v_ref[...], preferred_element_type=F32)


def _peer(hn, eidx, gate, u_b, v_b, x):
    T, D = x.shape
    E = u_b.shape[0]
    nk = PEER_N_KEYS
    tt = _tile(T, 256)
    te = 4 * nk
    grp = 8
    assert tt % grp == 0
    row = lambda i, e: (i, 0)
    exp = lambda i, e: (e, 0)
    slots = eidx.shape[1]
    return pl.pallas_call(
        functools.partial(_peer_kernel, grp=grp),
        out_shape=jax.ShapeDtypeStruct((T, D), F32),
        grid=(T // tt, E // te),
        in_specs=[pl.BlockSpec((tt, D), row), pl.BlockSpec((tt, slots), row), pl.BlockSpec((tt, slots), row),
                  pl.BlockSpec((tt, slots), row), pl.BlockSpec((te, D), exp), pl.BlockSpec((te, D), exp),
                  pl.BlockSpec((tt, D), row)],
        out_specs=pl.BlockSpec((tt, D), row),
        scratch_shapes=[pltpu.VMEM((nk, tt, nk), F32), pltpu.VMEM((grp, nk, nk), F32)],
        compiler_params=_params("parallel", "arbitrary"),
        name="peer",
    )(hn, eidx // nk, eidx % nk, gate, u_b, v_b, x)


def _layer(layer, xp, xs, mem_prompt, moba_k_pool, moba_v_pool, latent_pool, c_mem_k, c_mem_v, page_table,
           norm_attn_g, norm_ffn_g, norm_mem_g, w_in, moba_q_norm_g, moba_k_norm_g,
           mla_cq_norm_g, mla_ckv_norm_g, w_mla_uq, w_mla_uk, w_mla_uv, mla_q_norm_g, mla_k_norm_g,
           w_mem_k, w_mem_v, mem_q_norm_g, mem_k_norm_g, w_moba_o, w_mla_o, w_mem_o, w_out,
           w_peer_q, peer_subkeys, peer_u, peer_v):
    B, S, D = xp.shape
    DB, Q, _ = xs.shape
    Tp, Ts = B * S, DB * Q
    n_pages = page_table.shape[1]
    past = n_pages * moba_k_pool.shape[2]
    slopes = _alibi_slopes(MOBA_HEADS)
    qa_w, kv_w = MOBA_HEADS * MOBA_HEAD_DIM, MOBA_KV_HEADS * MOBA_HEAD_DIM
    qm_w = MEM_HEADS * MEM_HEAD_DIM
    cuts = np.cumsum([qa_w, kv_w, kv_w, MLA_Q_RANK, MLA_KV_RANK, MLA_ROPE_DIM, qm_w])
    c_qa, c_ka, c_va, c_cq, c_ckv, c_kpe, c_qm = [int(c) for c in cuts]

    x = jnp.concatenate([xp.reshape(Tp, D), xs.reshape(Ts, D)], axis=0)
    pos = jnp.concatenate([jnp.tile(jnp.arange(S), B), jnp.tile(past + jnp.arange(Q), DB)])
    t1, t2 = _rope_tables(pos)

    hn = _headnorm(x, norm_attn_g, D, BF16)
    w_attn = jnp.concatenate([w_in[:, :c_ckv], w_in[:, c_kpe:c_qm]], axis=1).astype(BF16)
    w_kpe2 = _rope_dup(w_in[:, c_ckv:c_kpe]).astype(BF16)
    pa = _mm(hn, w_attn)
    kpe2 = _mm(hn, w_kpe2)
    gates = _mm(hn, w_in[:, c_qm:].astype(BF16))
    qa = _headnorm(pa[:, :c_qa], moba_q_norm_g, MOBA_HEAD_DIM, F32)
    ka = _headnorm(pa[:, c_qa:c_ka], moba_k_norm_g, MOBA_HEAD_DIM, F32)
    va = pa[:, c_ka:c_va]
    cq = _headnorm(pa[:, c_va:c_cq], mla_cq_norm_g, MLA_Q_RANK, BF16)
    ckv = _headnorm(pa[:, c_cq:c_ckv], mla_ckv_norm_g, MLA_KV_RANK, F32)
    qm = _headnorm(pa[:, c_ckv:], mem_q_norm_g, MEM_HEAD_DIM, BF16)
    kpe = kpe2[:, :MLA_ROPE_DIM]
    lat = jnp.concatenate([ckv, kpe], axis=-1)

    h_idx = np.arange(MLA_HEADS)[:, None] * MLA_QK_DIM
    half = MLA_ROPE_DIM // 2
    r_idx = MLA_NOPE_DIM + np.concatenate([np.arange(MLA_ROPE_DIM), np.arange(half, MLA_ROPE_DIM), np.arange(half)])
    cols = (h_idx + np.concatenate([np.arange(MLA_NOPE_DIM), r_idx])[None, :]).reshape(-1)
    qcat = _mla_q_prep(_mm(cq, w_mla_uq[:, cols].astype(BF16)), t1, t2, mla_q_norm_g)
    wuk_b, wuv_b = w_mla_uk.astype(BF16), w_mla_uv.astype(BF16)

    o_a = _moba_prompt(qa[:Tp].reshape(B, S, qa_w), ka[:Tp].reshape(B, S, kv_w), va[:Tp].reshape(B, S, kv_w), slopes)
    kvp = _mm(ckv[:Tp].astype(BF16), jnp.concatenate([wuk_b, wuv_b], axis=1))
    kb, vb = _mla_kv_prep(kvp, kpe2[:Tp], t1[:Tp], t2[:Tp], mla_k_norm_g)
    o_b = _attn(qcat[:Tp].reshape(B, S, -1), kb.reshape(B, S, -1), vb.reshape(B, S, -1), heads=MLA_HEADS,
                dk=MLA_QK_PAD, dv=MLA_V_DIM, scale=MLA_QK_DIM ** -0.5, causal=True)
    M = mem_prompt.shape[1]
    mn = _headnorm(mem_prompt.reshape(B * M, D), norm_mem_g, D, BF16)
    mkv = _mm(mn, jnp.concatenate([w_mem_k, w_mem_v], axis=1).astype(BF16))
    mk = _headnorm(mkv[:, :qm_w], mem_k_norm_g, MEM_HEAD_DIM, F32)
    mv = mkv[:, qm_w:]
    mem_scale = MEM_HEAD_DIM ** -0.5
    o_m = _attn(qm[:Tp].reshape(B, S, qm_w), mk.reshape(B, M, qm_w), mv.reshape(B, M, qm_w), heads=MEM_HEADS,
                dk=MEM_HEAD_DIM, dv=MEM_HEAD_DIM, scale=mem_scale, causal=False)

    o_a_s = _moba_sample(qa[Tp:].reshape(DB, Q, qa_w), ka[Tp:].reshape(DB, Q, kv_w), va[Tp:].reshape(DB, Q, kv_w),
                         moba_k_pool, moba_v_pool, layer, page_table, slopes)
    o_b_s = _mla_sample(qcat[Tp:].reshape(DB, Q, -1), lat[Tp:].reshape(DB, Q, -1), latent_pool, layer, page_table,
                        wuk_b, wuv_b, mla_k_norm_g)
    qpad = 8
    qm_s = jnp.pad(qm[Tp:].reshape(DB, Q, qm_w), ((0, 0), (0, qpad - Q), (0, 0)))
    Ms = c_mem_k.shape[1]
    o_m_s = _attn(qm_s, c_mem_k.reshape(DB, Ms, qm_w), c_mem_v.reshape(DB, Ms, qm_w), heads=MEM_HEADS,
                  dk=MEM_HEAD_DIM, dv=MEM_HEAD_DIM, scale=mem_scale, causal=False)[:, :Q]

    oa = jnp.concatenate([o_a.reshape(Tp, -1), o_a_s.reshape(Ts, -1).astype(BF16)], axis=0)
    ob = jnp.concatenate([o_b.reshape(Tp, -1), o_b_s], axis=0)
    om = jnp.concatenate([o_m.reshape(Tp, -1), o_m_s.reshape(Ts, -1)], axis=0)
    mg = _merge(oa, ob, om, w_moba_o.astype(BF16), w_mla_o.astype(BF16), w_mem_o.astype(BF16), gates)
    x1 = _mm(mg, w_out.astype(BF16), residual=x)

    hf = _headnorm(x1, norm_ffn_g, D, BF16)
    eidx, gate = _peer_route(_mm(hf, w_peer_q.astype(BF16)), peer_subkeys)
    y = _peer(hf, eidx, gate, peer_u.astype(BF16), peer_v.astype(BF16), x1)

    kvh = (MOBA_KV_HEADS, MOBA_HEAD_DIM)
    states = (ka[:Tp].reshape(B, S, *kvh), va[:Tp].reshape(B, S, *kvh), lat[:Tp].reshape(B, S, -1),
              mk.reshape(B, M, MEM_HEADS, MEM_HEAD_DIM), mv.reshape(B, M, MEM_HEADS, MEM_HEAD_DIM),
              ka[Tp:].reshape(DB, Q, *kvh), va[Tp:].reshape(DB, Q, *kvh), lat[Tp:].reshape(DB, Q, -1))
    return y[:Tp].reshape(B, S, D), y[Tp:].reshape(DB, Q, D), states


def kernel(x_prompt, x_sample, mem_prompt, cache_moba_k, cache_moba_v, cache_mla_latent, cache_mem_k, cache_mem_v, page_table, norm_attn_g, norm_ffn_g, norm_mem_g, w_in, moba_q_norm_g, moba_k_norm_g, mla_cq_norm_g, mla_ckv_norm_g, w_mla_uq, w_mla_uk, w_mla_uv, mla_q_norm_g, mla_k_norm_g, w_mem_k, w_mem_v, mem_q_norm_g, mem_k_norm_g, w_moba_o, w_mla_o, w_mem_o, w_out, w_peer_q, peer_subkeys, peer_u, peer_v):
    xp, xs = x_prompt, x_sample
    new = [[] for _ in range(8)]
    for l in range(w_in.shape[0]):
        xp, xs, states = _layer(
            l, xp, xs, mem_prompt, cache_moba_k, cache_moba_v, cache_mla_latent, cache_mem_k[l],
            cache_mem_v[l], page_table, norm_attn_g[l], norm_ffn_g[l], norm_mem_g[l], w_in[l],
            moba_q_norm_g[l], moba_k_norm_g[l], mla_cq_norm_g[l], mla_ckv_norm_g[l], w_mla_uq[l], w_mla_uk[l],
            w_mla_uv[l], mla_q_norm_g[l], mla_k_norm_g[l], w_mem_k[l], w_mem_v[l], mem_q_norm_g[l],
            mem_k_norm_g[l], w_moba_o[l], w_mla_o[l], w_mem_o[l], w_out[l], w_peer_q[l], peer_subkeys[l],
            peer_u[l], peer_v[l])
        for lst, val in zip(new, states):
            lst.append(val)
    return (xp, xs) + tuple(jnp.stack(v) for v in new)
```

```python
import functools
import math

import numpy as np
import jax
import jax.numpy as jnp
from jax import lax
from jax.experimental import pallas as pl
from jax.experimental.pallas import tpu as pltpu

MOBA_HEADS = 12
MOBA_KV_HEADS = 2
MOBA_HEAD_DIM = 128
MOBA_BLOCK = 256
MOBA_TOPK = 3
MLA_HEADS = 12
MLA_Q_RANK = 768
MLA_KV_RANK = 512
MLA_NOPE_DIM = 128
MLA_ROPE_DIM = 64
MLA_QK_DIM = MLA_NOPE_DIM + MLA_ROPE_DIM
MLA_V_DIM = 128
ROPE_THETA = 10000.0
MEM_HEADS = 4
MEM_HEAD_DIM = 256
PEER_HEADS = 8
PEER_N_KEYS = 128
PEER_QUERY_DIM = 256
PEER_TOPK = 16
RMS_EPS = 1e-6

LANES = 128
MLA_QK_PAD = 2 * LANES
VMEM_LIMIT = 56 * 1024 * 1024
NEG = -1e30
RING_SLOTS = 4
HI = lax.Precision.HIGHEST
F32 = jnp.float32
BF16 = jnp.bfloat16
NT_DIMS = (((1,), (1,)), ((), ()))


def _tile(n, pref, align=8):
    t = (min(pref, n) // align) * align
    while t >= align:
        if n % t == 0:
            return t
        t -= align
    return n


def _params(*sem):
    return pltpu.CompilerParams(dimension_semantics=sem, vmem_limit_bytes=VMEM_LIMIT)


def _nt(a, b, precision=None):
    return lax.dot_general(a, b, NT_DIMS, precision=precision, preferred_element_type=F32)


def _headnorm_kernel(x_ref, g_ref, o_ref, *, hd):
    g = g_ref[...]
    for h in range(x_ref.shape[1] // hd):
        x = x_ref[:, h * hd:(h + 1) * hd]
        inv = lax.rsqrt(jnp.mean(x * x, axis=-1, keepdims=True) + RMS_EPS)
        o_ref[:, h * hd:(h + 1) * hd] = (x * inv * g).astype(o_ref.dtype)


def _headnorm(x, g, hd, out_dtype):
    T, W = x.shape
    tm = _tile(T, 256)
    return pl.pallas_call(
        functools.partial(_headnorm_kernel, hd=hd),
        out_shape=jax.ShapeDtypeStruct((T, W), out_dtype),
        grid=(T // tm,),
        in_specs=[pl.BlockSpec((tm, W), lambda i: (i, 0)), pl.BlockSpec((1, hd), lambda i: (0, 0))],
        out_specs=pl.BlockSpec((tm, W), lambda i: (i, 0)),
        compiler_params=_params("parallel"),
        name="headnorm",
    )(x, g.reshape(1, hd).astype(F32))


def _mm_kernel(a_ref, w_ref, o_ref):
    o_ref[...] = jnp.dot(a_ref[...], w_ref[...], preferred_element_type=F32).astype(o_ref.dtype)


def _mm_res_kernel(a_ref, w_ref, r_ref, o_ref):
    o_ref[...] = r_ref[...] + jnp.dot(a_ref[...], w_ref[...], preferred_element_type=F32)


def _mm(a, w, *, residual=None, tm=512, tn=512, out_dtype=F32):
    M, K = a.shape
    N = w.shape[1]
    tm, tn = _tile(M, tm), _tile(N, tn, LANES)
    in_specs = [pl.BlockSpec((tm, K), lambda i, j: (i, 0)), pl.BlockSpec((K, tn), lambda i, j: (0, j))]
    args = [a, w]
    body = _mm_kernel
    if residual is not None:
        in_specs.append(pl.BlockSpec((tm, tn), lambda i, j: (i, j)))
        args.append(residual)
        body = _mm_res_kernel
    return pl.pallas_call(
        body,
        out_shape=jax.ShapeDtypeStruct((M, N), out_dtype),
        grid=(M // tm, N // tn),
        in_specs=in_specs,
        out_specs=pl.BlockSpec((tm, tn), lambda i, j: (i, j)),
        compiler_params=_params("parallel", "parallel"),
        name="mm",
    )(*args)


def _rope_tables(pos):
    half = MLA_ROPE_DIM // 2
    inv = ROPE_THETA ** (-jnp.arange(half, dtype=F32) / half)
    ang = pos.astype(F32)[:, None] * inv
    cos, sin = jnp.cos(ang), jnp.sin(ang)
    z = jnp.zeros_like(cos)
    return jnp.concatenate([cos, cos, z, z], axis=-1), jnp.concatenate([-sin, sin, z, z], axis=-1)


def _rope_dup(v):
    half = MLA_ROPE_DIM // 2
    return jnp.concatenate([v, v[..., half:], v[..., :half]], axis=-1)


def _mla_q_kernel(q_ref, t1_ref, t2_ref, gn_ref, g2_ref, o_ref):
    t1, t2, gn, g2 = t1_ref[...], t2_ref[...], gn_ref[...], g2_ref[...]
    first = lax.broadcasted_iota(jnp.int32, t1.shape, 1) < MLA_ROPE_DIM
    for h in range(MLA_HEADS):
        lo = h * MLA_QK_PAD
        n = q_ref[:, lo:lo + LANES]
        r = q_ref[:, lo + LANES:lo + MLA_QK_PAD]
        ss = (jnp.sum(n * n, axis=-1, keepdims=True)
              + jnp.sum(jnp.where(first, r * r, 0.0), axis=-1, keepdims=True))
        inv = lax.rsqrt(ss * (1.0 / MLA_QK_DIM) + RMS_EPS)
        rn = r * inv * g2
        o_ref[:, lo:lo + LANES] = n * inv * gn
        o_ref[:, lo + LANES:lo + MLA_QK_PAD] = rn * t1 + pltpu.roll(rn, MLA_ROPE_DIM, axis=1) * t2


def _mla_q_prep(qraw, t1, t2, g):
    T, W = qraw.shape
    tm = _tile(T, 256)
    row = lambda i: (i, 0)
    fix = lambda i: (0, 0)
    return pl.pallas_call(
        _mla_q_kernel,
        out_shape=jax.ShapeDtypeStruct((T, W), F32),
        grid=(T // tm,),
        in_specs=[pl.BlockSpec((tm, W), row), pl.BlockSpec((tm, LANES), row), pl.BlockSpec((tm, LANES), row),
                  pl.BlockSpec((1, LANES), fix), pl.BlockSpec((1, LANES), fix)],
        out_specs=pl.BlockSpec((tm, W), row),
        compiler_params=_params("parallel"),
        name="mla_q_prep",
    )(qraw, t1, t2, g[:MLA_NOPE_DIM].reshape(1, LANES), _rope_dup(g[MLA_NOPE_DIM:]).reshape(1, LANES))


def _mla_kv_kernel(kv_ref, kpe_ref, t1_ref, t2_ref, gn_ref, g2_ref, k_ref, v_ref):
    t1, t2, gn, g2 = t1_ref[...], t2_ref[...], gn_ref[...], g2_ref[...]
    kpe = kpe_ref[...]
    first = lax.broadcasted_iota(jnp.int32, t1.shape, 1) < MLA_ROPE_DIM
    sspe = jnp.sum(jnp.where(first, kpe * kpe, 0.0), axis=-1, keepdims=True)
    kr = kpe * g2
    kr = kr * t1 + pltpu.roll(kr, MLA_ROPE_DIM, axis=1) * t2
    nk = MLA_HEADS * MLA_NOPE_DIM
    for h in range(MLA_HEADS):
        n = kv_ref[:, h * LANES:(h + 1) * LANES]
        inv = lax.rsqrt((jnp.sum(n * n, axis=-1, keepdims=True) + sspe) * (1.0 / MLA_QK_DIM) + RMS_EPS)
        lo = h * MLA_QK_PAD
        k_ref[:, lo:lo + LANES] = (n * inv * gn).astype(k_ref.dtype)
        k_ref[:, lo + LANES:lo + MLA_QK_PAD] = (kr * inv).astype(k_ref.dtype)
    v_ref[...] = kv_ref[:, nk:].astype(v_ref.dtype)


def _mla_kv_prep(kv, kpe2, t1, t2, g):
    T = kv.shape[0]
    tm = _tile(T, 256)
    row = lambda i: (i, 0)
    fix = lambda i: (0, 0)
    kw, vw = MLA_HEADS * MLA_QK_PAD, MLA_HEADS * MLA_V_DIM
    return pl.pallas_call(
        _mla_kv_kernel,
        out_shape=(jax.ShapeDtypeStruct((T, kw), BF16), jax.ShapeDtypeStruct((T, vw), BF16)),
        grid=(T // tm,),
        in_specs=[pl.BlockSpec((tm, kv.shape[1]), row), pl.BlockSpec((tm, LANES), row),
                  pl.BlockSpec((tm, LANES), row), pl.BlockSpec((tm, LANES), row),
                  pl.BlockSpec((1, LANES), fix), pl.BlockSpec((1, LANES), fix)],
        out_specs=(pl.BlockSpec((tm, kw), row), pl.BlockSpec((tm, vw), row)),
        compiler_params=_params("parallel"),
        name="mla_kv_prep",
    )(kv, kpe2, t1, t2, g[:MLA_NOPE_DIM].reshape(1, LANES), _rope_dup(g[MLA_NOPE_DIM:]).reshape(1, LANES))


def _attn_kernel(q_ref, k_ref, v_ref, o_ref, *, scale, causal):
    q = q_ref[0].astype(BF16)
    k = k_ref[0].astype(BF16)
    lg = _nt(q, k) * scale
    if causal:
        tq = q.shape[0]
        qpos = pl.program_id(2) * tq + lax.broadcasted_iota(jnp.int32, lg.shape, 0)
        kpos = lax.broadcasted_iota(jnp.int32, lg.shape, 1)
        lg = jnp.where(kpos <= qpos, lg, -jnp.inf)
    m = jnp.max(lg, axis=-1, keepdims=True)
    p = jnp.exp(lg - m)
    l = jnp.sum(p, axis=-1, keepdims=True)
    o = jnp.dot(p.astype(BF16), v_ref[0].astype(BF16), preferred_element_type=F32)
    o_ref[0] = (o / l).astype(o_ref.dtype)


def _attn(q, k, v, *, heads, dk, dv, scale, causal, out_dtype=BF16, tq=256):
    B, S, _ = q.shape
    Sk = k.shape[1]
    tq = _tile(S, tq)
    return pl.pallas_call(
        functools.partial(_attn_kernel, scale=scale, causal=causal),
        out_shape=jax.ShapeDtypeStruct((B, S, heads * dv), out_dtype),
        grid=(B, heads, S // tq),
        in_specs=[pl.BlockSpec((1, tq, dk), lambda b, h, i: (b, i, h)),
                  pl.BlockSpec((1, Sk, dk), lambda b, h, i: (b, 0, h)),
                  pl.BlockSpec((1, Sk, dv), lambda b, h, i: (b, 0, h))],
        out_specs=pl.BlockSpec((1, tq, dv), lambda b, h, i: (b, i, h)),
        compiler_params=_params("parallel", "parallel", "parallel"),
        name="attn",
    )(q, k, v)


def _mem_cache_kernel(q_ref, k_ref, v_ref, o_ref, *, scale, heads, hd):
    nt = hd // LANES
    rpt = heads * nt
    m_tok = k_ref.shape[1] // rpt
    q = q_ref[0]

    def rows(ref, h, t):
        return ref[0, pl.ds(t * heads + h, m_tok, stride=rpt), :].astype(BF16)

    for h in range(heads):
        lanes = [slice(h * hd + t * LANES, h * hd + (t + 1) * LANES) for t in range(nt)]
        lg = sum(_nt(q[:, lanes[t]], rows(k_ref, h, t)) for t in range(nt)) * scale
        m = jnp.max(lg, axis=-1, keepdims=True)
        p = jnp.exp(lg - m)
        l = jnp.sum(p, axis=-1, keepdims=True)
        pb = p.astype(BF16)
        for t in range(nt):
            o = jnp.dot(pb, rows(v_ref, h, t), preferred_element_type=F32)
            o_ref[0, :, lanes[t]] = (o / l).astype(o_ref.dtype)


def _mem_cache_attn(q, k_cache, v_cache, layer, *, scale):
    DB, qp, w = q.shape
    _, _, M, H, hd = k_cache.shape
    nt = hd // LANES

    def view(c):
        return c.reshape(-1, DB, M, H, nt, LANES).transpose(0, 1, 2, 4, 3, 5).reshape(-1, M * H * nt, LANES)
    kv_spec = pl.BlockSpec((1, M * H * nt, LANES), lambda b: (layer * DB + b, 0, 0))
    return pl.pallas_call(
        functools.partial(_mem_cache_kernel, scale=scale, heads=H, hd=hd),
        out_shape=jax.ShapeDtypeStruct((DB, qp, w), BF16),
        grid=(DB,),
        in_specs=[pl.BlockSpec((1, qp, w), lambda b: (b, 0, 0)), kv_spec, kv_spec],
        out_specs=pl.BlockSpec((1, qp, w), lambda b: (b, 0, 0)),
        compiler_params=_params("parallel"),
        name="mem_cache_attn",
    )(q, view(k_cache), view(v_cache))


def _alibi_slopes(n):
    def pow2(m):
        start = 2.0 ** (-8.0 / m)
        return [start ** (i + 1) for i in range(m)]
    p = 2 ** int(math.floor(math.log2(n)))
    s = pow2(p)
    if p < n:
        s = s + pow2(2 * p)[0::2][: n - p]
    return np.array(s, dtype=np.float32)


def _top_blocks(gate, lane, k):
    width = gate.shape[1]
    picks = []
    for _ in range(k):
        m = jnp.max(gate, axis=-1, keepdims=True)
        idx = jnp.min(jnp.where(gate == m, lane, width), axis=-1, keepdims=True)
        idx = jnp.where(m > -jnp.inf, idx, -1)
        picks.append(idx)
        gate = jnp.where(lane == idx, -jnp.inf, gate)
    return picks


def _moba_prompt_kernel(q_ref, k_ref, v_ref, slope_ref, o_ref, *, scale):
    qf = q_ref[0]
    kf = k_ref[0]
    tq, S = qf.shape[0], kf.shape[0]
    nblk = S // MOBA_BLOCK
    kmean = jnp.mean(kf.reshape(nblk, MOBA_BLOCK, MOBA_HEAD_DIM), axis=1)
    kmean = jnp.concatenate([kmean, jnp.zeros((LANES - nblk, MOBA_HEAD_DIM), F32)], axis=0)
    gate = _nt(qf, kmean, HI)
    q0 = pl.program_id(2) * tq
    own = (q0 + lax.broadcasted_iota(jnp.int32, (tq, 1), 0)) // MOBA_BLOCK
    lane = lax.broadcasted_iota(jnp.int32, gate.shape, 1)
    gate = jnp.where(lane < own, gate, -jnp.inf)
    picks = _top_blocks(gate, lane, MOBA_TOPK)

    lg = _nt(qf.astype(BF16), kf.astype(BF16)) * scale
    qpos = q0 + lax.broadcasted_iota(jnp.int32, lg.shape, 0)
    kpos = lax.broadcasted_iota(jnp.int32, lg.shape, 1)
    slope = slope_ref[0, 0:1, 0:1]
    lg = lg - slope * (qpos - kpos).astype(F32)
    kblk = kpos // MOBA_BLOCK
    allowed = (kblk == own) & (kpos <= qpos)
    for idx in picks:
        allowed = allowed | (kblk == idx)
    lg = jnp.where(allowed, lg, -jnp.inf)
    m = jnp.max(lg, axis=-1, keepdims=True)
    p = jnp.exp(lg - m)
    l = jnp.sum(p, axis=-1, keepdims=True)
    o = jnp.dot(p.astype(BF16), v_ref[0].astype(BF16), preferred_element_type=F32)
    o_ref[0] = (o / l).astype(o_ref.dtype)


def _moba_prompt(q, k, v, slopes):
    B, S, _ = q.shape
    assert S % MOBA_BLOCK == 0 and S // MOBA_BLOCK <= LANES
    tq = _tile(S, 256)
    grp = MOBA_HEADS // MOBA_KV_HEADS
    d = MOBA_HEAD_DIM
    return pl.pallas_call(
        functools.partial(_moba_prompt_kernel, scale=d ** -0.5),
        out_shape=jax.ShapeDtypeStruct((B, S, MOBA_HEADS * d), BF16),
        grid=(B, MOBA_HEADS, S // tq),
        in_specs=[pl.BlockSpec((1, tq, d), lambda b, h, i: (b, i, h)),
                  pl.BlockSpec((1, S, d), lambda b, h, i: (b, 0, h // grp)),
                  pl.BlockSpec((1, S, d), lambda b, h, i: (b, 0, h // grp)),
                  pl.BlockSpec((1, 8, LANES), lambda b, h, i: (h, 0, 0))],
        out_specs=pl.BlockSpec((1, tq, d), lambda b, h, i: (b, i, h)),
        compiler_params=_params("parallel", "parallel", "parallel"),
        name="moba_prompt",
    )(q, k, v, jnp.broadcast_to(jnp.asarray(slopes)[:, None, None], (MOBA_HEADS, 8, LANES)))


def _page_copies(pool, pt_ref, b, base, chunk, buf, slot, sem, ppc, rows):
    out = []
    for p in range(ppc):
        row0 = pl.multiple_of((base + pt_ref[b, chunk * ppc + p]) * rows, rows)
        out.append(pltpu.make_async_copy(pool.at[pl.ds(row0, rows)], buf.at[slot, pl.ds(p * rows, rows)],
                                         sem.at[slot]))
    return out


def _ring_prime(copies, nch, ns):
    for c in range(min(ns - 1, nch)):
        for cp in copies(chunk=c, slot=c % ns):
            cp.start()


def _ring_run(copies, nch, ns, compute, carry):
    def body(c, carry):
        slot = c % ns
        for cp in copies(chunk=c, slot=slot):
            cp.wait()

        @pl.when(c + ns - 1 < nch)
        def _():
            for cp in copies(chunk=c + ns - 1, slot=(c + ns - 1) % ns):
                cp.start()
        return compute(c, slot, carry)
    return lax.fori_loop(0, nch, body, carry)


def _moba_sample_kernel(pt_ref, q_ref, slope_ref, qidx_ref, kn_ref, vn_ref, kpool, vpool, o_ref,
                        buf, kb, ksum, sem, *, base, n_pages, ppc, page, n_new, scale):
    b = pl.program_id(0)
    nseq = pl.num_programs(0)
    ns = buf.shape[0]
    nch = n_pages // ppc
    ctok = ppc * page
    cblk = ctok // MOBA_BLOCK
    past = n_pages * page
    nb = past // MOBA_BLOCK
    d = MOBA_HEAD_DIM
    G = MOBA_KV_HEADS

    def copies(pool, seq):
        return functools.partial(_page_copies, pool, pt_ref, seq, base, buf=buf, sem=sem, ppc=ppc, rows=page * G)

    def head_rows(slot, g):
        return buf[slot, pl.ds(g, ctok, stride=G), :]

    def keys(c, slot, carry):
        for g in range(G):
            x = head_rows(slot, g)
            ksum[g, c] = jnp.sum(x.reshape(cblk, MOBA_BLOCK, d), axis=1)
            kb[g, pl.ds(pl.multiple_of(c * ctok, ctok), ctok), :] = x.astype(BF16)
        return carry

    @pl.when(b == 0)
    def _():
        _ring_prime(copies(kpool, b), nch, ns)
    _ring_run(copies(kpool, b), nch, ns, keys, 0)
    _ring_prime(copies(vpool, b), nch, ns)

    rows = q_ref.shape[2]
    lane = lax.broadcasted_iota(jnp.int32, (rows, LANES), 1)
    qidx = qidx_ref[:, 0:1]
    qs, slopes, picks = [], [], []
    for g in range(G):
        kmean = ksum[g].reshape(nb, d) * (1.0 / MOBA_BLOCK)
        kmean = jnp.concatenate([kmean, jnp.zeros((LANES - nb, d), F32)], axis=0)
        qf = q_ref[0, g]
        gate = _nt(qf, kmean, HI)
        gate = jnp.where(lane < nb, gate, -jnp.inf)
        picks.append(_top_blocks(gate, lane, MOBA_TOPK))
        qs.append(qf)
        slopes.append(slope_ref[g][:, 0:1])

    def values(c, slot, carry):
        kpos = c * ctok + lax.broadcasted_iota(jnp.int32, (rows, ctok), 1)
        kblk = kpos // MOBA_BLOCK
        out = []
        for g in range(G):
            m, l, acc = carry[g]
            kc = kb[g, pl.ds(pl.multiple_of(c * ctok, ctok), ctok), :]
            lg = _nt(qs[g].astype(BF16), kc) * scale
            lg = lg - slopes[g] * (past + qidx - kpos).astype(F32)
            allowed = (kblk == picks[g][0]) | (kblk == picks[g][1]) | (kblk == picks[g][2])
            m_new = jnp.maximum(m, jnp.max(jnp.where(allowed, lg, NEG), axis=-1, keepdims=True))
            p = jnp.where(allowed, jnp.exp(lg - m_new), 0.0)
            alpha = jnp.exp(m - m_new)
            l = l * alpha + jnp.sum(p, axis=-1, keepdims=True)
            acc = acc * alpha + jnp.dot(p.astype(BF16), head_rows(slot, g).astype(BF16),
                                        preferred_element_type=F32)
            out.append((m_new, l, acc))
        return tuple(out)
    init = tuple((jnp.full((rows, 1), NEG, F32), jnp.zeros((rows, 1), F32), jnp.zeros((rows, d), F32))
                 for _ in range(MOBA_KV_HEADS))
    state = _ring_run(copies(vpool, b), nch, ns, values, init)

    @pl.when(b + 1 < nseq)
    def _():
        _ring_prime(copies(kpool, b + 1), nch, ns)

    for g in range(MOBA_KV_HEADS):
        m, l, acc = state[g]
        for j in range(n_new):
            kj = kn_ref[0, j:j + 1, g * d:(g + 1) * d]
            vj = vn_ref[0, j:j + 1, g * d:(g + 1) * d]
            lg = jnp.sum(qs[g] * kj, axis=-1, keepdims=True) * scale - slopes[g] * (qidx - j).astype(F32)
            ok = qidx >= j
            m_new = jnp.maximum(m, jnp.where(ok, lg, NEG))
            p = jnp.where(ok, jnp.exp(lg - m_new), 0.0)
            alpha = jnp.exp(m - m_new)
            l = l * alpha + p
            acc = acc * alpha + p * vj
            m = m_new
        o_ref[0, g] = acc / l


def _moba_sample(q, k_new, v_new, k_pool, v_pool, layer, page_table, slopes):
    DB, Q, _ = q.shape
    n_phys, page = k_pool.shape[1:3]
    n_pages = page_table.shape[1]
    past = n_pages * page
    assert past % MOBA_BLOCK == 0 and MOBA_TOPK <= past // MOBA_BLOCK <= LANES
    d, grp = MOBA_HEAD_DIM, MOBA_HEADS // MOBA_KV_HEADS
    kvw = MOBA_KV_HEADS * d
    ppc = _tile(n_pages, 16, 2 * MOBA_BLOCK // page)
    rows = Q * grp
    qg = q.reshape(DB, Q, MOBA_KV_HEADS, grp, d).transpose(0, 2, 1, 3, 4).reshape(DB, MOBA_KV_HEADS, rows, d)
    slope_rows = np.broadcast_to(np.tile(slopes.reshape(MOBA_KV_HEADS, 1, grp), (1, Q, 1)).reshape(
        MOBA_KV_HEADS, rows, 1), (MOBA_KV_HEADS, rows, LANES))
    qidx_rows = np.broadcast_to(np.repeat(np.arange(Q, dtype=np.int32), grp)[:, None], (rows, LANES))
    fix2 = lambda b, pt: (0, 0)
    out = pl.pallas_call(
        functools.partial(_moba_sample_kernel, base=layer * n_phys, n_pages=n_pages, ppc=ppc, page=page, n_new=Q,
                          scale=d ** -0.5),
        out_shape=jax.ShapeDtypeStruct((DB, MOBA_KV_HEADS, rows, d), F32),
        grid_spec=pltpu.PrefetchScalarGridSpec(
            num_scalar_prefetch=1,
            grid=(DB,),
            in_specs=[pl.BlockSpec((1, MOBA_KV_HEADS, rows, d), lambda b, pt: (b, 0, 0, 0)),
                      pl.BlockSpec((MOBA_KV_HEADS, rows, LANES), lambda b, pt: (0, 0, 0)),
                      pl.BlockSpec((rows, LANES), fix2),
                      pl.BlockSpec((1, Q, kvw), lambda b, pt: (b, 0, 0)),
                      pl.BlockSpec((1, Q, kvw), lambda b, pt: (b, 0, 0)),
                      pl.BlockSpec(memory_space=pl.ANY),
                      pl.BlockSpec(memory_space=pl.ANY)],
            out_specs=pl.BlockSpec((1, MOBA_KV_HEADS, rows, d), lambda b, pt: (b, 0, 0, 0)),
            scratch_shapes=[pltpu.VMEM((RING_SLOTS, ppc * page * MOBA_KV_HEADS, d), F32),
                            pltpu.VMEM((MOBA_KV_HEADS, past, d), BF16),
                            pltpu.VMEM((MOBA_KV_HEADS, n_pages // ppc, ppc * page // MOBA_BLOCK, d), F32),
                            pltpu.SemaphoreType.DMA((RING_SLOTS,))]),
        compiler_params=_params("arbitrary"),
        name="moba_sample",
    )(page_table, qg, jnp.asarray(slope_rows), jnp.asarray(qidx_rows), k_new, v_new,
      k_pool.reshape(-1, d), v_pool.reshape(-1, d))
    return out.reshape(DB, MOBA_KV_HEADS, Q, grp, d).transpose(0, 2, 1, 3, 4).reshape(DB, Q, MOBA_HEADS * d)


MLA_HEAD_PAD = 16


def _mla_sample_kernel(pt_ref, qw_ref, latn_ref, csn_ref, qidx_ref, wukt_ref, cs_hbm, pool, o_ref,
                       buf, csbuf, lhs, sem, cssem, *, base, n_pages, ppc, page, n_new, scale):
    b = pl.program_id(0)
    nseq = pl.num_programs(0)
    ns = buf.shape[0]
    nch = n_pages // ppc
    ctok = ppc * page
    r, rd, H = MLA_KV_RANK, MLA_ROPE_DIM, MLA_HEADS
    latw = r + rd
    qw = qw_ref[0]
    q_cs = qw[:, r:]
    rows = qw.shape[0]
    nq = rows // MLA_HEAD_PAD
    nkn = H * MLA_NOPE_DIM

    @pl.when(b == 0)
    def _():
        lhs[0:nkn, :] = wukt_ref[...]
    lhs[nkn:, :] = qw[:, :r]

    def copies(seq):
        def chunk_copies(chunk, slot):
            return (_page_copies(pool, pt_ref, seq, base, chunk, buf, slot, sem, ppc, latw)
                    + [pltpu.make_async_copy(cs_hbm.at[chunk], csbuf.at[slot], cssem.at[slot])])
        return chunk_copies

    def scores(ct, kpet, cs):
        tk = ct.shape[1]
        cb = ct.astype(BF16)
        prod = jnp.dot(lhs[...], cb, preferred_element_type=F32)
        knt = prod[:nkn]
        ssq = jnp.sum((knt * knt).reshape(H, MLA_NOPE_DIM, tk), axis=1) + jnp.sum(kpet * kpet, axis=0, keepdims=True)
        inv = lax.rsqrt(ssq * (1.0 / MLA_QK_DIM) + RMS_EPS)
        inv = jnp.concatenate([inv, jnp.zeros((MLA_HEAD_PAD - H, tk), F32)], axis=0)
        inv = jnp.concatenate([inv] * nq, axis=0)
        krot = (jnp.concatenate([kpet, kpet], axis=0) * cs).astype(BF16)
        raw = prod[nkn:] + jnp.dot(q_cs, krot, preferred_element_type=F32)
        return cb, raw * inv * scale

    def attend(ct, kpet, cs, allowed, carry):
        m, l, acc = carry
        cb, lg = scores(ct, kpet, cs)
        m_new = jnp.maximum(m, jnp.max(lg if allowed is None else jnp.where(allowed, lg, NEG), axis=-1, keepdims=True))
        p = jnp.exp(lg - m_new)
        if allowed is not None:
            p = jnp.where(allowed, p, 0.0)
        alpha = jnp.exp(m - m_new)
        return (m_new, l * alpha + jnp.sum(p, axis=-1, keepdims=True), acc * alpha + _nt(p.astype(BF16), cb))

    def page_attend(c, slot, carry):
        ct = jnp.concatenate([buf[slot, pl.ds(p * latw, r), :] for p in range(ppc)], axis=1)
        kpet = jnp.concatenate([buf[slot, pl.ds(p * latw + r, rd), :] for p in range(ppc)], axis=1)
        return attend(ct, kpet, csbuf[slot], None, carry)

    @pl.when(b == 0)
    def _():
        _ring_prime(copies(b), nch, ns)
    init = (jnp.full((rows, 1), NEG, F32), jnp.zeros((rows, 1), F32), jnp.zeros((rows, r), F32))
    state = _ring_run(copies(b), nch, ns, page_attend, init)

    @pl.when(b + 1 < nseq)
    def _():
        _ring_prime(copies(b + 1), nch, ns)

    tn = latn_ref.shape[2]
    j = lax.broadcasted_iota(jnp.int32, (rows, tn), 1)
    allowed = (j <= qidx_ref[:, 0:1]) & (j < n_new)
    m, l, acc = attend(latn_ref[0, :r, :], latn_ref[0, r:, :], csn_ref[...], allowed, state)
    o_ref[0] = acc / l


def _rope_cs(pos):
    half = MLA_ROPE_DIM // 2
    inv = ROPE_THETA ** (-jnp.arange(half, dtype=F32) / half)
    ang = pos.astype(F32)[:, None] * inv
    cos, sin = jnp.cos(ang), jnp.sin(ang)
    return jnp.concatenate([cos, cos], axis=-1), jnp.concatenate([sin, sin], axis=-1)


def _abs_q_kernel(q_ref, g_ref, w_ref, o_ref):
    qg = (q_ref[...] * g_ref[...]).astype(BF16)
    o_ref[0] = _nt(qg, w_ref[...]).astype(o_ref.dtype)


def _head_out_kernel(a_ref, w_ref, o_ref):
    o_ref[...] = jnp.dot(a_ref[0], w_ref[...], preferred_element_type=F32).astype(o_ref.dtype)


def _mla_sample(qcat, lat_new, latent_pool, layer, page_table, wuk_b, wuv_b, gk):
    DB, Q, _ = qcat.shape
    _, n_phys, page, latw = latent_pool.shape
    n_pages = page_table.shape[1]
    past = n_pages * page
    H, r, nd, rd = MLA_HEADS, MLA_KV_RANK, MLA_NOPE_DIM, MLA_ROPE_DIM
    HP = MLA_HEAD_PAD
    M = DB * Q
    rows = Q * HP
    assert page == LANES and H <= HP
    q4 = qcat.reshape(M, H, MLA_QK_PAD)
    q_abs = pl.pallas_call(
        _abs_q_kernel,
        out_shape=jax.ShapeDtypeStruct((H, M, r), BF16),
        grid=(H,),
        in_specs=[pl.BlockSpec((M, nd), lambda h: (0, 2 * h)), pl.BlockSpec((1, nd), lambda h: (0, 0)),
                  pl.BlockSpec((r, nd), lambda h: (0, h))],
        out_specs=pl.BlockSpec((1, M, r), lambda h: (h, 0, 0)),
        compiler_params=_params("parallel"),
        name="mla_abs_q",
    )(qcat.reshape(M, H * MLA_QK_PAD), gk[:nd].reshape(1, nd), wuk_b)
    qr = q4[:, :, nd:nd + rd]
    gr = gk[nd:]
    q_cos = qr * gr
    q_sin = jnp.concatenate([qr[..., rd // 2:], -qr[..., :rd // 2]], axis=-1) * gr
    qw = jnp.concatenate([q_abs.transpose(1, 0, 2), q_cos.astype(BF16), q_sin.astype(BF16)], axis=-1)
    qw = jnp.pad(qw, ((0, 0), (0, HP - H), (0, 0))).reshape(DB, rows, r + 2 * rd)
    tn = LANES
    latn = jnp.pad(lat_new.transpose(0, 2, 1), ((0, 0), (0, 0), (0, tn - Q)))
    cs = jnp.concatenate(_rope_cs(jnp.arange(past)), axis=-1).T
    csn = jnp.concatenate(_rope_cs(past + jnp.arange(tn)), axis=-1).T
    qidx = np.broadcast_to(np.repeat(np.arange(Q, dtype=np.int32), HP)[:, None], (rows, LANES))
    pool_t = jnp.swapaxes(latent_pool, 2, 3).reshape(-1, page)
    ppc = _tile(n_pages, 8, 1)
    nch, ctok = n_pages // ppc, ppc * page
    cs = cs.reshape(2 * rd, nch, ctok).transpose(1, 0, 2)
    fix2 = lambda b, pt: (0, 0)
    acc = pl.pallas_call(
        functools.partial(_mla_sample_kernel, base=layer * n_phys, n_pages=n_pages, ppc=ppc, page=page, n_new=Q,
                          scale=MLA_QK_DIM ** -0.5),
        out_shape=jax.ShapeDtypeStruct((DB, rows, r), F32),
        grid_spec=pltpu.PrefetchScalarGridSpec(
            num_scalar_prefetch=1,
            grid=(DB,),
            in_specs=[pl.BlockSpec((1, rows, r + 2 * rd), lambda b, pt: (b, 0, 0)),
                      pl.BlockSpec((1, latw, tn), lambda b, pt: (b, 0, 0)),
                      pl.BlockSpec((2 * rd, tn), fix2),
                      pl.BlockSpec((rows, LANES), fix2),
                      pl.BlockSpec((H * nd, r), fix2),
                      pl.BlockSpec(memory_space=pl.ANY),
                      pl.BlockSpec(memory_space=pl.ANY)],
            out_specs=pl.BlockSpec((1, rows, r), lambda b, pt: (b, 0, 0)),
            scratch_shapes=[pltpu.VMEM((RING_SLOTS, ppc * latw, page), F32),
                            pltpu.VMEM((RING_SLOTS, 2 * rd, ctok), F32),
                            pltpu.VMEM((H * nd + rows, r), BF16),
                            pltpu.SemaphoreType.DMA((RING_SLOTS,)),
                            pltpu.SemaphoreType.DMA((RING_SLOTS,))]),
        compiler_params=_params("arbitrary"),
        name="mla_sample",
    )(page_table, qw, latn, csn, jnp.asarray(qidx), wuk_b.T, cs, pool_t)
    a = acc.reshape(DB, Q, HP, r)[:, :, :H].transpose(2, 0, 1, 3).reshape(H, M, r).astype(BF16)
    return pl.pallas_call(
        _head_out_kernel,
        out_shape=jax.ShapeDtypeStruct((M, H * MLA_V_DIM), BF16),
        grid=(H,),
        in_specs=[pl.BlockSpec((1, M, r), lambda h: (h, 0, 0)), pl.BlockSpec((r, MLA_V_DIM), lambda h: (0, h))],
        out_specs=pl.BlockSpec((M, MLA_V_DIM), lambda h: (0, h)),
        compiler_params=_params("parallel"),
        name="mla_head_out",
    )(a, wuv_b)


def _merge_kernel(oa_ref, ob_ref, om_ref, wa_ref, wb_ref, wm_ref, ga_ref, gb_ref, gm_ref, o_ref):
    def branch(o, w, g):
        return jax.nn.sigmoid(g[...]) * jnp.dot(o[...], w[...], preferred_element_type=F32)
    o_ref[...] = (branch(oa_ref, wa_ref, ga_ref) + branch(ob_ref, wb_ref, gb_ref)
                  + branch(om_ref, wm_ref, gm_ref)).astype(o_ref.dtype)


def _merge(oa, ob, om, wa, wb, wm, gates):
    T, D = oa.shape[0], wa.shape[1]
    tm, tn = _tile(T, 512), _tile(D, 512, LANES)
    nj = D // tn
    row = lambda i, j: (i, 0)
    col = lambda i, j: (0, j)
    return pl.pallas_call(
        _merge_kernel,
        out_shape=jax.ShapeDtypeStruct((T, D), BF16),
        grid=(T // tm, nj),
        in_specs=[pl.BlockSpec((tm, oa.shape[1]), row), pl.BlockSpec((tm, ob.shape[1]), row),
                  pl.BlockSpec((tm, om.shape[1]), row),
                  pl.BlockSpec((wa.shape[0], tn), col), pl.BlockSpec((wb.shape[0], tn), col),
                  pl.BlockSpec((wm.shape[0], tn), col),
                  pl.BlockSpec((tm, tn), lambda i, j: (i, j)),
                  pl.BlockSpec((tm, tn), lambda i, j: (i, j + nj)),
                  pl.BlockSpec((tm, tn), lambda i, j: (i, j + 2 * nj))],
        out_specs=pl.BlockSpec((tm, tn), lambda i, j: (i, j)),
        compiler_params=_params("parallel", "parallel"),
        name="merge",
    )(oa, ob, om, wa, wb, wm, gates, gates, gates)


def _top_rows(s, row, k):
    n = s.shape[0]
    vals, idxs = [], []
    for _ in range(k):
        m = jnp.max(s, axis=0, keepdims=True)
        i = jnp.min(jnp.where(s == m, row, n), axis=0, keepdims=True)
        vals.append(m)
        idxs.append(i)
        s = jnp.where(row == i, -jnp.inf, s)
    return vals, idxs


def _peer_pair_counts(k):
    return [k // (a + 1) for a in range(k)]


def _peer_route_kernel(q_ref, sk_ref, pos_ref, e_ref, g_ref):
    nk, k = PEER_N_KEYS, PEER_TOPK
    tm = q_ref.shape[0]
    row = lax.broadcasted_iota(jnp.int32, (nk, tm), 0)
    sv, si = [], []
    for p in range(2):
        s = _nt(sk_ref[0, p], q_ref[:, p * LANES:(p + 1) * LANES], HI)
        vals, idxs = _top_rows(s, row, k)
        sv.append(vals)
        si.append(idxs)
    sv1 = jnp.concatenate(sv[1], axis=0)
    si1 = jnp.concatenate(si[1], axis=0)
    counts = _peer_pair_counts(k)
    pad = pos_ref.shape[0] - sum(counts)
    cand = jnp.concatenate([sv[0][a] + sv1[:counts[a]] for a in range(k)]
                           + [jnp.full((pad, tm), -jnp.inf, F32)], axis=0)
    cidx = jnp.concatenate([si[0][a] * nk + si1[:counts[a]] for a in range(k)]
                           + [jnp.zeros((pad, tm), jnp.int32)], axis=0)
    pos = pos_ref[...]
    tv, te = [], []
    for _ in range(k):
        m = jnp.max(cand, axis=0, keepdims=True)
        j = jnp.min(jnp.where(cand == m, pos, k * k), axis=0, keepdims=True)
        pick = pos == j
        te.append(jnp.max(jnp.where(pick, cidx, -1), axis=0, keepdims=True))
        tv.append(m)
        cand = jnp.where(pick, -jnp.inf, cand)
    ts = jnp.concatenate(tv, axis=0)
    ex = jnp.exp(ts - tv[0])
    e_ref[0] = jnp.concatenate(te, axis=0)
    g_ref[0] = ex / jnp.sum(ex, axis=0, keepdims=True)


def _peer_route(qp, subkeys):
    T = qp.shape[0]
    tm = _tile(T, 256, LANES) if T % LANES == 0 else T
    nt = T // tm
    slots = PEER_HEADS * PEER_TOPK
    k = PEER_TOPK
    pair_pos = [a * k + b for a, n in enumerate(_peer_pair_counts(k)) for b in range(n)]
    npos = -(-len(pair_pos) // 8) * 8
    pos = np.full((npos, tm), k * k, np.int32)
    pos[:len(pair_pos)] = np.asarray(pair_pos, np.int32)[:, None]
    e, g = pl.pallas_call(
        _peer_route_kernel,
        out_shape=(jax.ShapeDtypeStruct((nt, slots, tm), jnp.int32), jax.ShapeDtypeStruct((nt, slots, tm), F32)),
        grid=(nt, PEER_HEADS),
        in_specs=[pl.BlockSpec((tm, PEER_QUERY_DIM), lambda i, h: (i, h)),
                  pl.BlockSpec((1, 2, PEER_N_KEYS, PEER_QUERY_DIM // 2), lambda i, h: (h, 0, 0, 0)),
                  pl.BlockSpec((npos, tm), lambda i, h: (0, 0))],
        out_specs=(pl.BlockSpec((1, PEER_TOPK, tm), lambda i, h: (i, h, 0)),
                   pl.BlockSpec((1, PEER_TOPK, tm), lambda i, h: (i, h, 0))),
        compiler_params=_params("parallel", "parallel"),
        name="peer_route",
    )(qp, subkeys, jnp.asarray(pos))
    return e.transpose(0, 2, 1).reshape(T, slots), g.transpose(0, 2, 1).reshape(T, slots)


WTMP_PITCH = PEER_N_KEYS + 8


def _peer_kernel(h_ref, a_ref, b_ref, g_ref, u_ref, v_ref, x_ref, o_ref, wbuf, wtmp, *, grp):
    e = pl.program_id(1)
    tt = h_ref.shape[0]
    nk = PEER_N_KEYS
    te = u_ref.shape[0]

    @pl.when(e == 0)
    def _():
        o_ref[...] = x_ref[...]
        sub = lax.broadcasted_iota(jnp.int32, (nk, a_ref.shape[1]), 0)

        def build(t8, carry):
            t0 = pl.multiple_of(t8 * grp, grp)
            for kk in range(grp):
                a = a_ref[pl.ds(t0 + kk, 1), :]
                bb = b_ref[pl.ds(t0 + kk, 1), :]
                gg = g_ref[pl.ds(t0 + kk, 1), :]
                ahot = jnp.where(sub == a, 1.0, 0.0).astype(BF16)
                bw = jnp.where(sub == bb, gg, 0.0)
                bhi = bw.astype(BF16)
                blo = (bw - bhi.astype(F32)).astype(BF16)
                wtmp[kk * WTMP_PITCH:kk * WTMP_PITCH + nk, :] = _nt(ahot, bhi) + _nt(ahot, blo)
            for i in range(nk):
                wbuf[i, pl.ds(t0, grp), :] = wtmp[pl.ds(i, grp, stride=WTMP_PITCH), :]
            return carry
        lax.fori_loop(0, tt // grp, build, 0)

    z = _nt(h_ref[...], u_ref[...])
    act = 0.5 * z * (1.0 + lax.erf(z * math.sqrt(0.5)))
    w = jnp.concatenate([wbuf[e * (te // nk) + r] for r in range(te // nk)], axis=1)
    o_ref[...] += jnp.dot((w * act).astype(BF16), v_ref[...], preferred_element_type=F32)


def _peer(hn, eidx, gate, u_b, v_b, x):
    T, D = x.shape
    E = u_b.shape[0]
    nk = PEER_N_KEYS
    tt = _tile(T, 256)
    te = 4 * nk
    grp = 8
    assert tt % grp == 0
    row = lambda i, e: (i, 0)
    exp = lambda i, e: (e, 0)
    slots = eidx.shape[1]
    return pl.pallas_call(
        functools.partial(_peer_kernel, grp=grp),
        out_shape=jax.ShapeDtypeStruct((T, D), F32),
        grid=(T // tt, E // te),
        in_specs=[pl.BlockSpec((tt, D), row), pl.BlockSpec((tt, slots), row), pl.BlockSpec((tt, slots), row),
                  pl.BlockSpec((tt, slots), row), pl.BlockSpec((te, D), exp), pl.BlockSpec((te, D), exp),
                  pl.BlockSpec((tt, D), row)],
        out_specs=pl.BlockSpec((tt, D), row),
        scratch_shapes=[pltpu.VMEM((nk, tt, nk), F32), pltpu.VMEM((grp * WTMP_PITCH, nk), F32)],
        compiler_params=_params("parallel", "arbitrary"),
        name="peer",
    )(hn, eidx // nk, eidx % nk, gate, u_b, v_b, x)


def _layer(layer, xp, xs, mem_prompt, moba_k_pool, moba_v_pool, latent_pool, mem_k_cache, mem_v_cache, page_table,
           norm_attn_g, norm_ffn_g, norm_mem_g, w_in, moba_q_norm_g, moba_k_norm_g,
           mla_cq_norm_g, mla_ckv_norm_g, w_mla_uq, w_mla_uk, w_mla_uv, mla_q_norm_g, mla_k_norm_g,
           w_mem_k, w_mem_v, mem_q_norm_g, mem_k_norm_g, w_moba_o, w_mla_o, w_mem_o, w_out,
           w_peer_q, peer_subkeys, peer_u, peer_v):
    B, S, D = xp.shape
    DB, Q, _ = xs.shape
    Tp, Ts = B * S, DB * Q
    n_pages = page_table.shape[1]
    past = n_pages * moba_k_pool.shape[2]
    slopes = _alibi_slopes(MOBA_HEADS)
    qa_w, kv_w = MOBA_HEADS * MOBA_HEAD_DIM, MOBA_KV_HEADS * MOBA_HEAD_DIM
    qm_w = MEM_HEADS * MEM_HEAD_DIM
    cuts = np.cumsum([qa_w, kv_w, kv_w, MLA_Q_RANK, MLA_KV_RANK, MLA_ROPE_DIM, qm_w])
    c_qa, c_ka, c_va, c_cq, c_ckv, c_kpe, c_qm = [int(c) for c in cuts]

    x = jnp.concatenate([xp.reshape(Tp, D), xs.reshape(Ts, D)], axis=0)
    pos = jnp.concatenate([jnp.tile(jnp.arange(S), B), jnp.tile(past + jnp.arange(Q), DB)])
    t1, t2 = _rope_tables(pos)

    hn = _headnorm(x, norm_attn_g, D, BF16)
    w_attn = jnp.concatenate([w_in[:, :c_ckv], w_in[:, c_kpe:c_qm]], axis=1).astype(BF16)
    w_kpe2 = _rope_dup(w_in[:, c_ckv:c_kpe]).astype(BF16)
    pa = _mm(hn, w_attn)
    kpe2 = _mm(hn, w_kpe2)
    gates = _mm(hn, w_in[:, c_qm:].astype(BF16))
    qa = _headnorm(pa[:, :c_qa], moba_q_norm_g, MOBA_HEAD_DIM, F32)
    ka = _headnorm(pa[:, c_qa:c_ka], moba_k_norm_g, MOBA_HEAD_DIM, F32)
    va = pa[:, c_ka:c_va]
    cq = _headnorm(pa[:, c_va:c_cq], mla_cq_norm_g, MLA_Q_RANK, BF16)
    ckv = _headnorm(pa[:, c_cq:c_ckv], mla_ckv_norm_g, MLA_KV_RANK, F32)
    qm = _headnorm(pa[:, c_ckv:], mem_q_norm_g, MEM_HEAD_DIM, BF16)
    kpe = kpe2[:, :MLA_ROPE_DIM]
    lat = jnp.concatenate([ckv, kpe], axis=-1)

    h_idx = np.arange(MLA_HEADS)[:, None] * MLA_QK_DIM
    half = MLA_ROPE_DIM // 2
    r_idx = MLA_NOPE_DIM + np.concatenate([np.arange(MLA_ROPE_DIM), np.arange(half, MLA_ROPE_DIM), np.arange(half)])
    cols = (h_idx + np.concatenate([np.arange(MLA_NOPE_DIM), r_idx])[None, :]).reshape(-1)
    qcat = _mla_q_prep(_mm(cq, w_mla_uq[:, cols].astype(BF16)), t1, t2, mla_q_norm_g)
    wuk_b, wuv_b = w_mla_uk.astype(BF16), w_mla_uv.astype(BF16)

    o_a = _moba_prompt(qa[:Tp].reshape(B, S, qa_w), ka[:Tp].reshape(B, S, kv_w), va[:Tp].reshape(B, S, kv_w), slopes)
    kvp = _mm(ckv[:Tp].astype(BF16), jnp.concatenate([wuk_b, wuv_b], axis=1))
    kb, vb = _mla_kv_prep(kvp, kpe2[:Tp], t1[:Tp], t2[:Tp], mla_k_norm_g)
    o_b = _attn(qcat[:Tp].reshape(B, S, -1), kb.reshape(B, S, -1), vb.reshape(B, S, -1), heads=MLA_HEADS,
                dk=MLA_QK_PAD, dv=MLA_V_DIM, scale=MLA_QK_DIM ** -0.5, causal=True)
    M = mem_prompt.shape[1]
    mn = _headnorm(mem_prompt.reshape(B * M, D), norm_mem_g, D, BF16)
    mkv = _mm(mn, jnp.concatenate([w_mem_k, w_mem_v], axis=1).astype(BF16))
    mk = _headnorm(mkv[:, :qm_w], mem_k_norm_g, MEM_HEAD_DIM, F32)
    mv = mkv[:, qm_w:]
    mem_scale = MEM_HEAD_DIM ** -0.5
    o_m = _attn(qm[:Tp].reshape(B, S, qm_w), mk.reshape(B, M, qm_w), mv.reshape(B, M, qm_w), heads=MEM_HEADS,
                dk=MEM_HEAD_DIM, dv=MEM_HEAD_DIM, scale=mem_scale, causal=False)

    o_a_s = _moba_sample(qa[Tp:].reshape(DB, Q, qa_w), ka[Tp:].reshape(DB, Q, kv_w), va[Tp:].reshape(DB, Q, kv_w),
                         moba_k_pool, moba_v_pool, layer, page_table, slopes)
    o_b_s = _mla_sample(qcat[Tp:].reshape(DB, Q, -1), lat[Tp:].reshape(DB, Q, -1), latent_pool, layer, page_table,
                        wuk_b, wuv_b, mla_k_norm_g)
    qpad = 8
    qm_s = jnp.pad(qm[Tp:].reshape(DB, Q, qm_w), ((0, 0), (0, qpad - Q), (0, 0)))
    o_m_s = _mem_cache_attn(qm_s, mem_k_cache, mem_v_cache, layer, scale=mem_scale)[:, :Q]

    oa = jnp.concatenate([o_a.reshape(Tp, -1), o_a_s.reshape(Ts, -1).astype(BF16)], axis=0)
    ob = jnp.concatenate([o_b.reshape(Tp, -1), o_b_s], axis=0)
    om = jnp.concatenate([o_m.reshape(Tp, -1), o_m_s.reshape(Ts, -1)], axis=0)
    mg = _merge(oa, ob, om, w_moba_o.astype(BF16), w_mla_o.astype(BF16), w_mem_o.astype(BF16), gates)
    x1 = _mm(mg, w_out.astype(BF16), residual=x)

    hf = _headnorm(x1, norm_ffn_g, D, BF16)
    eidx, gate = _peer_route(_mm(hf, w_peer_q.astype(BF16)), peer_subkeys)
    y = _peer(hf, eidx, gate, peer_u.astype(BF16), peer_v.astype(BF16), x1)

    kvh = (MOBA_KV_HEADS, MOBA_HEAD_DIM)
    states = (ka[:Tp].reshape(B, S, *kvh), va[:Tp].reshape(B, S, *kvh), lat[:Tp].reshape(B, S, -1),
              mk.reshape(B, M, MEM_HEADS, MEM_HEAD_DIM), mv.reshape(B, M, MEM_HEADS, MEM_HEAD_DIM),
              ka[Tp:].reshape(DB, Q, *kvh), va[Tp:].reshape(DB, Q, *kvh), lat[Tp:].reshape(DB, Q, -1))
    return y[:Tp].reshape(B, S, D), y[Tp:].reshape(DB, Q, D), states


def kernel(x_prompt, x_sample, mem_prompt, cache_moba_k, cache_moba_v, cache_mla_latent, cache_mem_k, cache_mem_v, page_table, norm_attn_g, norm_ffn_g, norm_mem_g, w_in, moba_q_norm_g, moba_k_norm_g, mla_cq_norm_g, mla_ckv_norm_g, w_mla_uq, w_mla_uk, w_mla_uv, mla_q_norm_g, mla_k_norm_g, w_mem_k, w_mem_v, mem_q_norm_g, mem_k_norm_g, w_moba_o, w_mla_o, w_mem_o, w_out, w_peer_q, peer_subkeys, peer_u, peer_v):
    xp, xs = x_prompt, x_sample
    new = [[] for _ in range(8)]
    for l in range(w_in.shape[0]):
        xp, xs, states = _layer(
            l, xp, xs, mem_prompt, cache_moba_k, cache_moba_v, cache_mla_latent, cache_mem_k,
            cache_mem_v, page_table, norm_attn_g[l], norm_ffn_g[l], norm_mem_g[l], w_in[l],
            moba_q_norm_g[l], moba_k_norm_g[l], mla_cq_norm_g[l], mla_ckv_norm_g[l], w_mla_uq[l], w_mla_uk[l],
            w_mla_uv[l], mla_q_norm_g[l], mla_k_norm_g[l], w_mem_k[l], w_mem_v[l], mem_q_norm_g[l],
            mem_k_norm_g[l], w_moba_o[l], w_mla_o[l], w_mem_o[l], w_out[l], w_peer_q[l], peer_subkeys[l],
            peer_u[l], peer_v[l])
        for lst, val in zip(new, states):
            lst.append(val)
    return (xp, xs) + tuple(jnp.stack(v) for v in new)
```
